```python
import jax, jax.numpy as jnp
from jax import lax
import numpy as np

D_MODEL = 2048
BATCH = 2
SEQ = 4096
DEPTH = 4
DEC_BATCH = 32
DEC_SEQ = 4
PAST_LEN = 16384
PAGE_SIZE = 128

N_A_LAYERS = DEPTH // 2
N_B_LAYERS = DEPTH - N_A_LAYERS
D_RNN = D_MODEL
RNN_BLOCK = 256
N_RNN_BLOCKS = D_RNN // RNN_BLOCK
RNN_CONV_W = 4
LRU_C = 8.0
N_HEADS = 32
HEAD_DIM = 64
N_KV_HEADS = 8
GROUP = N_HEADS // N_KV_HEADS
WINDOW = 128
ATTN_BLOCK = 128
ROPE_THETA = 10000.0
D_FF = 3 * D_MODEL
FFN_CONV_W = 3
EPS = 1e-6

kernel_name = "yoco_hawk_swa_sink_convffn_step"


def rms_norm(x, g):
    xf = x.astype(jnp.float32)
    y = xf * lax.rsqrt(jnp.mean(xf * xf, axis=-1, keepdims=True) + EPS)
    return (y * g.astype(jnp.float32)).astype(x.dtype)


def causal_dwconv(x, prev, w, b):
    width = w.shape[0]
    t_len = x.shape[1]
    xp = jnp.concatenate([prev.astype(x.dtype), x], axis=1)
    y = b + w[0] * xp[:, 0:t_len]
    for k in range(1, width):
        y = y + w[k] * xp[:, k:k + t_len]
    return y, xp[:, t_len:]


def rg_lru(u, h0, wa, ba, wx, bx, lam):
    bsz, t_len, _ = u.shape
    ub = u.reshape(bsz, t_len, N_RNN_BLOCKS, RNN_BLOCK)
    gate_a = jnp.einsum("btni,nij->btnj", ub, wa).reshape(bsz, t_len, D_RNN) + ba
    gate_x = jnp.einsum("btni,nij->btnj", ub, wx).reshape(bsz, t_len, D_RNN) + bx
    r = jax.nn.sigmoid(gate_a.astype(jnp.float32))
    i = jax.nn.sigmoid(gate_x.astype(jnp.float32))
    log_a = -LRU_C * r * jax.nn.softplus(-lam.astype(jnp.float32))
    a = jnp.exp(log_a)
    mult = jnp.sqrt(-jnp.expm1(2.0 * log_a))
    b = mult * i * u.astype(jnp.float32)

    def combine(left, right):
        a1, b1 = left
        a2, b2 = right
        return a1 * a2, a2 * b1 + b2

    a_cum, b_cum = lax.associative_scan(combine, (a, b), axis=1)
    h = a_cum * h0.astype(jnp.float32)[:, None, :] + b_cum
    return h.astype(u.dtype), h[:, -1].astype(u.dtype)


def recurrent_block(h, h0, conv_prev, w_in, conv_w, conv_b, wa, ba, wx, bx, lam, w_out):
    gu = jnp.einsum("btd,de->bte", h, w_in)
    gate, u = jnp.split(gu, 2, axis=-1)
    u_c, conv_buf = causal_dwconv(u, conv_prev, conv_w, conv_b)
    y, h_last = rg_lru(u_c, h0, wa, ba, wx, bx, lam)
    out = jnp.einsum("btr,rd->btd", jax.nn.gelu(gate) * y, w_out)
    return out, h_last, conv_buf


def conv_ffn(h, prev, w_up, conv_w, conv_b, w_down):
    gu = jnp.einsum("btd,df->btf", h, w_up)
    g, up = jnp.split(gu, 2, axis=-1)
    g_c, buf = causal_dwconv(g, prev, conv_w, conv_b)
    return jnp.einsum("btf,fd->btd", jax.nn.gelu(g_c) * up, w_down), buf


def rope(x, pos):
    half = HEAD_DIM // 2
    freqs = ROPE_THETA ** (-jnp.arange(half, dtype=jnp.float32) / half)
    ang = pos.astype(jnp.float32)[:, None] * freqs[None, :]
    cos = jnp.cos(ang)[None, :, None, :]
    sin = jnp.sin(ang)[None, :, None, :]
    xf = x.astype(jnp.float32)
    x1, x2 = xf[..., :half], xf[..., half:]
    return jnp.concatenate([x1 * cos - x2 * sin, x2 * cos + x1 * sin], axis=-1).astype(x.dtype)


def shared_kv(x, pos, g, w_k, w_v):
    bsz, t_len, _ = x.shape
    h = rms_norm(x, g)
    k = jnp.einsum("btd,de->bte", h, w_k).reshape(bsz, t_len, N_KV_HEADS, HEAD_DIM)
    v = jnp.einsum("btd,de->bte", h, w_v).reshape(bsz, t_len, N_KV_HEADS, HEAD_DIM)
    return rope(k, pos), v


def queries(h, pos, w_q):
    bsz, t_len, _ = h.shape
    q = jnp.einsum("btd,de->bte", h, w_q).reshape(bsz, t_len, N_HEADS, HEAD_DIM)
    return rope(q, pos)


def to_bands(k):
    bsz, s_len = k.shape[:2]
    kb = k.reshape(bsz, s_len // ATTN_BLOCK, ATTN_BLOCK, N_KV_HEADS, HEAD_DIM)
    prev = jnp.pad(kb[:, :-1], ((0, 0), (1, 0), (0, 0), (0, 0), (0, 0)))
    return jnp.concatenate([prev, kb], axis=2)


def window_sink_attention(q, k, v, q_pos, k_pos, sinks):
    bsz, nb, tq = q.shape[:3]
    qg = q.reshape(bsz, nb, tq, N_KV_HEADS, GROUP, HEAD_DIM)
    s = jnp.einsum("bnqkgd,bnskd->bnkgqs", qg, k).astype(jnp.float32) * (HEAD_DIM ** -0.5)
    diff = q_pos[:, :, None] - k_pos[:, None, :]
    allowed = (diff >= 0) & (diff < WINDOW) & (k_pos[:, None, :] >= 0)
    s = jnp.where(allowed[None, :, None, None], s, -jnp.inf)
    sink = sinks.astype(jnp.float32).reshape(N_KV_HEADS, GROUP)[None, None, :, :, None, None]
    m = jnp.maximum(jnp.max(s, axis=-1, keepdims=True), sink)
    p = jnp.exp(s - m)
    p = p / (jnp.sum(p, axis=-1, keepdims=True) + jnp.exp(sink - m))
    o = jnp.einsum("bnkgqs,bnskd->bnqkgd", p.astype(v.dtype), v)
    return o.reshape(bsz, nb * tq, N_HEADS * HEAD_DIM)


def setup_inputs(seed: int = 0) -> dict:
    key = jax.random.key(seed)
    ks = iter(jax.random.split(key, 40))
    f32 = jnp.float32

    def nrm(shape, scale):
        return jax.random.normal(next(ks), shape, f32) * scale

    def gain(shape):
        return 1.0 + nrm(shape, 0.02)

    win_buf = min(WINDOW, PAST_LEN)
    a0 = jax.random.uniform(next(ks), (N_A_LAYERS, D_RNN), f32, 0.9, 0.999)
    return {
        "x_prompt": nrm((BATCH, SEQ, D_MODEL), 1.0),
        "x_sample": nrm((DEC_BATCH, DEC_SEQ, D_MODEL), 1.0),
        "state_rglru_h": nrm((N_A_LAYERS, DEC_BATCH, D_RNN), 0.5),
        "state_rglru_conv": nrm((N_A_LAYERS, DEC_BATCH, RNN_CONV_W - 1, D_RNN), 1.0),
        "state_ffn_conv": nrm((DEPTH, DEC_BATCH, FFN_CONV_W - 1, D_FF), 1.0),
        "cache_window_k": nrm((DEC_BATCH, win_buf, N_KV_HEADS, HEAD_DIM), 1.0),
        "cache_window_v": nrm((DEC_BATCH, win_buf, N_KV_HEADS, HEAD_DIM), 1.0),
        "a_norm": gain((N_A_LAYERS, D_MODEL)),
        "a_w_in": nrm((N_A_LAYERS, D_MODEL, 2 * D_RNN), D_MODEL ** -0.5),
        "a_conv_w": nrm((N_A_LAYERS, RNN_CONV_W, D_RNN), RNN_CONV_W ** -0.5),
        "a_conv_b": nrm((N_A_LAYERS, D_RNN), 0.01),
        "a_gate_a_w": nrm((N_A_LAYERS, N_RNN_BLOCKS, RNN_BLOCK, RNN_BLOCK), RNN_BLOCK ** -0.5),
        "a_gate_a_b": nrm((N_A_LAYERS, D_RNN), 0.01),
        "a_gate_x_w": nrm((N_A_LAYERS, N_RNN_BLOCKS, RNN_BLOCK, RNN_BLOCK), RNN_BLOCK ** -0.5),
        "a_gate_x_b": nrm((N_A_LAYERS, D_RNN), 0.01),
        "a_lambda": jnp.log(a0) - jnp.log1p(-a0),
        "a_w_out": nrm((N_A_LAYERS, D_RNN, D_MODEL), D_RNN ** -0.5),
        "kv_norm": gain((D_MODEL,)),
        "w_k": nrm((D_MODEL, N_KV_HEADS * HEAD_DIM), D_MODEL ** -0.5),
        "w_v": nrm((D_MODEL, N_KV_HEADS * HEAD_DIM), D_MODEL ** -0.5),
        "b_norm": gain((N_B_LAYERS, D_MODEL)),
        "w_q": nrm((N_B_LAYERS, D_MODEL, N_HEADS * HEAD_DIM), D_MODEL ** -0.5),
        "sinks": nrm((N_B_LAYERS, N_HEADS), 0.5),
        "w_o": nrm((N_B_LAYERS, N_HEADS * HEAD_DIM, D_MODEL), (N_HEADS * HEAD_DIM) ** -0.5),
        "f_norm": gain((DEPTH, D_MODEL)),
        "f_w_up": nrm((DEPTH, D_MODEL, 2 * D_FF), D_MODEL ** -0.5),
        "f_conv_w": nrm((DEPTH, FFN_CONV_W, D_FF), FFN_CONV_W ** -0.5),
        "f_conv_b": nrm((DEPTH, D_FF), 0.01),
        "f_w_down": nrm((DEPTH, D_FF, D_MODEL), D_FF ** -0.5),
        "final_norm": gain((D_MODEL,)),
    }


def reference(x_prompt, x_sample, state_rglru_h, state_rglru_conv, state_ffn_conv,
              cache_window_k, cache_window_v,
              a_norm, a_w_in, a_conv_w, a_conv_b, a_gate_a_w, a_gate_a_b,
              a_gate_x_w, a_gate_x_b, a_lambda, a_w_out,
              kv_norm, w_k, w_v, b_norm, w_q, sinks, w_o,
              f_norm, f_w_up, f_conv_w, f_conv_b, f_w_down, final_norm):
    bp, s_len, _ = x_prompt.shape
    bs, t_len, _ = x_sample.shape
    win_buf = cache_window_k.shape[1]
    win_p = min(WINDOW, s_len)
    n_blk = s_len // ATTN_BLOCK
    pos_p = jnp.arange(s_len, dtype=jnp.int32)
    pos_s = PAST_LEN + jnp.arange(t_len, dtype=jnp.int32)
    q_pos_p = pos_p.reshape(n_blk, ATTN_BLOCK)
    k_pos_p = ((jnp.arange(n_blk, dtype=jnp.int32)[:, None] - 1) * ATTN_BLOCK
               + jnp.arange(2 * ATTN_BLOCK, dtype=jnp.int32)[None, :])
    q_pos_s = pos_s[None, :]
    k_pos_s = (PAST_LEN - win_buf + jnp.arange(win_buf + t_len, dtype=jnp.int32))[None, :]

    xp, xs = x_prompt, x_sample
    h_p_list, h_s_list, c_p_list, c_s_list, f_p_list, f_s_list = [], [], [], [], [], []
    for layer in range(DEPTH):
        if layer < N_A_LAYERS:
            i = layer
            prm = (a_w_in[i], a_conv_w[i], a_conv_b[i], a_gate_a_w[i], a_gate_a_b[i],
                   a_gate_x_w[i], a_gate_x_b[i], a_lambda[i], a_w_out[i])
            h0_p = jnp.zeros((bp, D_RNN), xp.dtype)
            c0_p = jnp.zeros((bp, RNN_CONV_W - 1, D_RNN), xp.dtype)
            yp, hp_last, cp_buf = recurrent_block(rms_norm(xp, a_norm[i]), h0_p, c0_p, *prm)
            ys, hs_last, cs_buf = recurrent_block(rms_norm(xs, a_norm[i]), state_rglru_h[i],
                                                  state_rglru_conv[i], *prm)
            xp = xp + yp
            xs = xs + ys
            h_p_list.append(hp_last)
            h_s_list.append(hs_last)
            c_p_list.append(cp_buf)
            c_s_list.append(cs_buf)
        else:
            j = layer - N_A_LAYERS
            if j == 0:
                k_p, v_p = shared_kv(xp, pos_p, kv_norm, w_k, w_v)
                k_s, v_s = shared_kv(xs, pos_s, kv_norm, w_k, w_v)
                kb_p = to_bands(k_p)
                vb_p = to_bands(v_p)
                k_all = jnp.concatenate([cache_window_k.astype(k_s.dtype), k_s], axis=1)
                v_all = jnp.concatenate([cache_window_v.astype(v_s.dtype), v_s], axis=1)
            qp = queries(rms_norm(xp, b_norm[j]), pos_p, w_q[j]).reshape(
                bp, n_blk, ATTN_BLOCK, N_HEADS, HEAD_DIM)
            op = window_sink_attention(qp, kb_p, vb_p, q_pos_p, k_pos_p, sinks[j])
            xp = xp + jnp.einsum("bte,ed->btd", op, w_o[j])
            qs = queries(rms_norm(xs, b_norm[j]), pos_s, w_q[j])[:, None]
            o_s = window_sink_attention(qs, k_all[:, None], v_all[:, None], q_pos_s, k_pos_s, sinks[j])
            xs = xs + jnp.einsum("bte,ed->btd", o_s, w_o[j])
        fprm = (f_w_up[layer], f_conv_w[layer], f_conv_b[layer], f_w_down[layer])
        f0_p = jnp.zeros((bp, FFN_CONV_W - 1, D_FF), xp.dtype)
        yp, fp_buf = conv_ffn(rms_norm(xp, f_norm[layer]), f0_p, *fprm)
        ys, fs_buf = conv_ffn(rms_norm(xs, f_norm[layer]), state_ffn_conv[layer], *fprm)
        xp = xp + yp
        xs = xs + ys
        f_p_list.append(fp_buf)
        f_s_list.append(fs_buf)

    y_prompt = rms_norm(xp, final_norm)
    y_sample = rms_norm(xs, final_norm)
    prompt_rglru_h = jnp.stack(h_p_list)
    sample_rglru_h = jnp.stack(h_s_list)
    prompt_rglru_conv = jnp.stack(c_p_list)
    sample_rglru_conv = jnp.stack(c_s_list)
    prompt_ffn_conv = jnp.stack(f_p_list)
    sample_ffn_conv = jnp.stack(f_s_list)
    prompt_window_k = k_p[:, s_len - win_p:]
    prompt_window_v = v_p[:, s_len - win_p:]
    sample_window_k = k_all[:, t_len:]
    sample_window_v = v_all[:, t_len:]
    return (y_prompt, y_sample, prompt_rglru_h, sample_rglru_h, prompt_rglru_conv, sample_rglru_conv,
            prompt_ffn_conv, sample_ffn_conv, prompt_window_k, prompt_window_v,
            sample_window_k, sample_window_v)
```

```python
import functools

import jax
import jax.numpy as jnp
from jax import lax
from jax.experimental import pallas as pl
from jax.experimental.pallas import tpu as pltpu

F32 = jnp.float32
BF16 = jnp.bfloat16

D_MODEL = 2048
PAST_LEN = 16384
D_RNN = 2048
RNN_BLOCK = 256
LRU_C = 8.0
N_HEADS = 32
HEAD_DIM = 64
N_KV_HEADS = 8
ATTN_BLOCK = 128
ROPE_THETA = 10000.0
D_FF = 3 * D_MODEL
EPS = 1e-6

LANES = 128
SUBLANES = 8
GROUP_ROWS = SUBLANES
STATE_ROWS = 4
KV_LANES = N_KV_HEADS * HEAD_DIM
PAIR_W = 4 * LANES
N_PAIRS = N_KV_HEADS // 2
VMEM_LIMIT = 52 * 1024 * 1024

TM_PROMPT = 512
TC_RNN = 512
TF_FFN = 512


def _rms(x, g):
    ms = jnp.mean(x * x, axis=-1, keepdims=True)
    return x * lax.rsqrt(ms + EPS) * g


def _dot(a, b):
    return jnp.dot(a, b, preferred_element_type=F32)


def _shifted(xe, k, drop):
    return pltpu.roll(xe, k, 0)[drop:]


def _rope(x, cos, sin_lo, sin_hi):
    w = x.shape[1]
    n = w // LANES
    cosw = jnp.concatenate([cos] * n, axis=1)
    lo = jnp.concatenate([sin_lo] * n, axis=1)
    hi = jnp.concatenate([sin_hi] * n, axis=1)
    half = HEAD_DIM // 2
    return x * cosw + pltpu.roll(x, w - half, 1) * lo + pltpu.roll(x, half, 1) * hi


def _ffn_kernel(*refs, tm, tf, n_chunks, tiles_per_seq, carry, final):
    it = iter(refs)
    x_ref, ng_ref, wg_ref, wu_ref, cw_ref, cb_ref, wd_ref = (next(it) for _ in range(7))
    prev_ref = None if carry else next(it)
    fn_ref = next(it) if final else None
    out_ref, tail_ref, h_s = next(it), next(it), next(it)
    tail_s = next(it) if carry else None
    i = pl.program_id(0)
    j = pl.program_id(1)

    @pl.when(j == 0)
    def _():
        x = x_ref[...]
        h_s[...] = _rms(x, ng_ref[...]).astype(BF16)
        out_ref[...] = x

    h = h_s[...]
    g = _dot(h, wg_ref[...])
    u = _dot(h, wu_ref[...])
    if carry:
        first = (i % tiles_per_seq) == 0

        @pl.when(first)
        def _():
            tail_s[j] = jnp.zeros((SUBLANES, tf), F32)

        ge = jnp.concatenate([tail_s[j], g], axis=0)
        drop = SUBLANES
    else:
        row = lax.broadcasted_iota(jnp.int32, (tm, tf), 0)
        g = jnp.where((row & (GROUP_ROWS - 1)) >= STATE_ROWS, g, prev_ref[...])
        ge = g
        drop = 0
    cw = cw_ref[...]
    gc = cb_ref[...] + cw[0:1] * _shifted(ge, 2, drop)
    gc = gc + cw[1:2] * _shifted(ge, 1, drop)
    gc = gc + cw[2:3] * g
    act = (jax.nn.gelu(gc) * u).astype(BF16)
    out_ref[...] += _dot(act, wd_ref[...])
    if carry:
        tail_s[j] = g[tm - SUBLANES:]
        tail_ref[0] = g[tm - SUBLANES:]
    else:
        tail_ref[0] = g
    if final:
        @pl.when(j == n_chunks - 1)
        def _():
            out_ref[...] = _rms(out_ref[...], fn_ref[...])


def _ffn_call(x, ng, w_up, cw, cb, w_down, prev, fn, *, tm, carry, tiles_per_seq):
    n_rows = x.shape[0]
    tf = TF_FFN
    n_chunks = D_FF // tf
    n_tiles = n_rows // tm
    tail_rows = SUBLANES if carry else tm
    final = fn is not None
    in_specs = [
        pl.BlockSpec((tm, D_MODEL), lambda i, j: (i, 0)),
        pl.BlockSpec((1, D_MODEL), lambda i, j: (0, 0)),
        pl.BlockSpec((D_MODEL, tf), lambda i, j: (0, j)),
        pl.BlockSpec((D_MODEL, tf), lambda i, j: (0, j + n_chunks)),
        pl.BlockSpec((3, tf), lambda i, j: (0, j)),
        pl.BlockSpec((1, tf), lambda i, j: (0, j)),
        pl.BlockSpec((tf, D_MODEL), lambda i, j: (j, 0)),
    ]
    args = [x, ng, w_up, w_up, cw, cb, w_down]
    if not carry:
        in_specs.append(pl.BlockSpec((tm, tf), lambda i, j: (i, j)))
        args.append(prev)
    if final:
        in_specs.append(pl.BlockSpec((1, D_MODEL), lambda i, j: (0, 0)))
        args.append(fn)
    scratch = [pltpu.VMEM((tm, D_MODEL), BF16)]
    if carry:
        scratch.append(pltpu.VMEM((n_chunks, SUBLANES, tf), F32))
    return pl.pallas_call(
        functools.partial(_ffn_kernel, tm=tm, tf=tf, n_chunks=n_chunks,
                          tiles_per_seq=tiles_per_seq, carry=carry, final=final),
        grid=(n_tiles, n_chunks),
        in_specs=in_specs,
        out_specs=[
            pl.BlockSpec((tm, D_MODEL), lambda i, j: (i, 0)),
            pl.BlockSpec((1, tail_rows, tf), lambda i, j: (i, 0, j)),
        ],
        out_shape=[
            jax.ShapeDtypeStruct((n_rows, D_MODEL), F32),
            jax.ShapeDtypeStruct((n_tiles, tail_rows, D_FF), F32),
        ],
        scratch_shapes=scratch,
        compiler_params=pltpu.CompilerParams(
            dimension_semantics=("arbitrary", "arbitrary"), vmem_limit_bytes=VMEM_LIMIT),
        name="conv_ffn_prompt" if carry else "conv_ffn_sample",
    )(*args)


def _rec_kernel(*refs, tm, tc, tiles_per_seq, carry):
    it = iter(refs)
    (x_ref, ng_ref, wgt_ref, wu_ref, cw_ref, cb_ref, wa_ref, ba_ref, wx_ref, bx_ref,
     lam_ref, wo_ref) = (next(it) for _ in range(12))
    prev_ref, h0_ref = (None, None) if carry else (next(it), next(it))
    out_ref, utail_ref, ytail_ref, h_s = (next(it) for _ in range(4))
    utail_s, hc_s, y_s = (next(it), next(it), next(it)) if carry else (None, None, None)
    i = pl.program_id(0)
    j = pl.program_id(1)

    @pl.when(j == 0)
    def _():
        x = x_ref[...]
        h_s[...] = _rms(x, ng_ref[...]).astype(BF16)
        out_ref[...] = x

    h = h_s[...]
    gate = _dot(h, wgt_ref[...])
    u = _dot(h, wu_ref[...])
    row8 = lax.broadcasted_iota(jnp.int32, (tm, tc), 0) & (GROUP_ROWS - 1)
    if carry:
        first = (i % tiles_per_seq) == 0

        @pl.when(first)
        def _():
            utail_s[j] = jnp.zeros((SUBLANES, tc), F32)
            hc_s[j] = jnp.zeros((SUBLANES, tc), F32)

        ue = jnp.concatenate([utail_s[j], u], axis=0)
        drop = SUBLANES
    else:
        is_token = row8 >= STATE_ROWS
        u = jnp.where(is_token, u, prev_ref[...])
        ue = u
        drop = 0
    cw = cw_ref[...]
    uc = cb_ref[...] + cw[0:1] * _shifted(ue, 3, drop)
    uc = uc + cw[1:2] * _shifted(ue, 2, drop)
    uc = uc + cw[2:3] * _shifted(ue, 1, drop)
    uc = uc + cw[3:4] * u

    ucb = uc.astype(BF16)
    nb = tc // RNN_BLOCK
    ga = jnp.concatenate(
        [_dot(ucb[:, n * RNN_BLOCK:(n + 1) * RNN_BLOCK], wa_ref[n]) for n in range(nb)], axis=1)
    gx = jnp.concatenate(
        [_dot(ucb[:, n * RNN_BLOCK:(n + 1) * RNN_BLOCK], wx_ref[n]) for n in range(nb)], axis=1)
    r = jax.nn.sigmoid(ga + ba_ref[...])
    ig = jax.nn.sigmoid(gx + bx_ref[...])
    z = -lam_ref[...]
    softplus = jnp.maximum(z, 0.0) + jnp.log1p(jnp.exp(-jnp.abs(z)))
    log_a = (-LRU_C * r) * softplus
    a = jnp.exp(log_a)
    b = jnp.sqrt(-jnp.tanh(log_a) * (a * a + 1.0)) * ig * uc

    if not carry:
        a = jnp.where(is_token, a, 0.0)
        b = jnp.where(is_token, b, h0_ref[...])
    for d in (1, 2, 4):
        m = row8 >= d
        a_sh = pltpu.roll(a, d, 0)
        b_sh = pltpu.roll(b, d, 0)
        b = jnp.where(m, a * b_sh + b, b)
        a = jnp.where(m, a * a_sh, a)
    if carry:
        hc = hc_s[j]
        for gi in range(tm // SUBLANES):
            rows = slice(gi * SUBLANES, (gi + 1) * SUBLANES)
            hg = a[rows] * hc + b[rows]
            y_s[rows, :] = hg
            hc = jnp.broadcast_to(hg[SUBLANES - 1:SUBLANES, :], (SUBLANES, tc))
        hc_s[j] = hc
        y = y_s[...]
    else:
        y = b
    act = (jax.nn.gelu(gate) * y).astype(BF16)
    out_ref[...] += _dot(act, wo_ref[...])
    if carry:
        utail_s[j] = u[tm - SUBLANES:]
        utail_ref[0] = u[tm - SUBLANES:]
        ytail_ref[0] = y[tm - SUBLANES:]
    else:
        utail_ref[0] = u
        ytail_ref[0] = y


def _rec_call(x, ng, w_in, cw, cb, wa, ba, wx, bx, lam, w_out, prev, h0, *, tm, carry,
              tiles_per_seq):
    n_rows = x.shape[0]
    tc = TC_RNN
    n_chunks = D_RNN // tc
    nb = tc // RNN_BLOCK
    n_tiles = n_rows // tm
    tail_rows = SUBLANES if carry else tm
    chunk = lambda i, j: (0, j)
    in_specs = [
        pl.BlockSpec((tm, D_MODEL), lambda i, j: (i, 0)),
        pl.BlockSpec((1, D_MODEL), lambda i, j: (0, 0)),
        pl.BlockSpec((D_MODEL, tc), chunk),
        pl.BlockSpec((D_MODEL, tc), lambda i, j: (0, j + n_chunks)),
        pl.BlockSpec((4, tc), chunk),
        pl.BlockSpec((1, tc), chunk),
        pl.BlockSpec((nb, RNN_BLOCK, RNN_BLOCK), lambda i, j: (j, 0, 0)),
        pl.BlockSpec((1, tc), chunk),
        pl.BlockSpec((nb, RNN_BLOCK, RNN_BLOCK), lambda i, j: (j, 0, 0)),
        pl.BlockSpec((1, tc), chunk),
        pl.BlockSpec((1, tc), chunk),
        pl.BlockSpec((tc, D_MODEL), lambda i, j: (j, 0)),
    ]
    args = [x, ng, w_in, w_in, cw, cb, wa, ba, wx, bx, lam, w_out]
    if not carry:
        in_specs += [pl.BlockSpec((tm, tc), lambda i, j: (i, j))] * 2
        args += [prev, h0]
    scratch = [pltpu.VMEM((tm, D_MODEL), BF16)]
    if carry:
        scratch += [pltpu.VMEM((n_chunks, SUBLANES, tc), F32),
                    pltpu.VMEM((n_chunks, SUBLANES, tc), F32),
                    pltpu.VMEM((tm, tc), F32)]
    tail_spec = pl.BlockSpec((1, tail_rows, tc), lambda i, j: (i, 0, j))
    tail_shape = jax.ShapeDtypeStruct((n_tiles, tail_rows, D_RNN), F32)
    return pl.pallas_call(
        functools.partial(_rec_kernel, tm=tm, tc=tc, tiles_per_seq=tiles_per_seq, carry=carry),
        grid=(n_tiles, n_chunks),
        in_specs=in_specs,
        out_specs=[pl.BlockSpec((tm, D_MODEL), lambda i, j: (i, 0)), tail_spec, tail_spec],
        out_shape=[jax.ShapeDtypeStruct((n_rows, D_MODEL), F32), tail_shape, tail_shape],
        scratch_shapes=scratch,
        compiler_params=pltpu.CompilerParams(
            dimension_semantics=("arbitrary", "arbitrary"), vmem_limit_bytes=VMEM_LIMIT),
        name="recurrent_prompt" if carry else "recurrent_sample",
    )(*args)


def _kv_kernel(x_ref, ng_ref, w_ref, cos_ref, slo_ref, shi_ref, k_ref, v_ref):
    h = _rms(x_ref[...], ng_ref[...]).astype(BF16)
    kv = _dot(h, w_ref[...])
    k_ref[...] = _rope(kv[:, :KV_LANES], cos_ref[...], slo_ref[...], shi_ref[...])
    v_ref[...] = kv[:, KV_LANES:]


def _kv_call(x, ng, w_kv, tables, *, tm):
    n_rows = x.shape[0]
    row = lambda i: (i, 0)
    fixed = lambda i: (0, 0)
    return pl.pallas_call(
        _kv_kernel,
        grid=(n_rows // tm,),
        in_specs=[
            pl.BlockSpec((tm, D_MODEL), row),
            pl.BlockSpec((1, D_MODEL), fixed),
            pl.BlockSpec((D_MODEL, 2 * KV_LANES), fixed),
            pl.BlockSpec((tm, LANES), row),
            pl.BlockSpec((tm, LANES), row),
            pl.BlockSpec((tm, LANES), row),
        ],
        out_specs=[pl.BlockSpec((tm, KV_LANES), row), pl.BlockSpec((tm, KV_LANES), row)],
        out_shape=[jax.ShapeDtypeStruct((n_rows, KV_LANES), F32)] * 2,
        compiler_params=pltpu.CompilerParams(
            dimension_semantics=("arbitrary",), vmem_limit_bytes=VMEM_LIMIT),
        name="shared_kv",
    )(x, ng, w_kv, *tables)


def _attend(q, kp, vp, allowed, sink_ref, pair):
    rows = q.shape[0]
    hi_half = lax.broadcasted_iota(jnp.int32, (rows, LANES), 1) >= HEAD_DIM
    blocks, heads, keeps = [], [], []
    for gl in range(4):
        p = gl // 2
        keep = hi_half if p else jnp.logical_not(hi_half)
        qg = q[:, gl * LANES:(gl + 1) * LANES]
        blocks.append(jnp.where(keep, qg, 0.0))
        blocks.append(jnp.where(keep, pltpu.roll(qg, HEAD_DIM, 1), 0.0))
        g = 4 * pair + gl
        heads += [2 * g + p, 2 * g + 1 - p]
        keeps.append(keep)
    qs = jnp.concatenate(blocks, axis=0).astype(BF16)
    s = lax.dot_general(qs, kp, (((1,), (1,)), ((), ())), preferred_element_type=F32)
    s = s * (HEAD_DIM ** -0.5)
    probs = []
    for bi in range(8):
        sb = jnp.where(allowed, s[bi * rows:(bi + 1) * rows], -jnp.inf)
        sink = sink_ref[heads[bi]]
        m = jnp.maximum(jnp.max(sb, axis=-1, keepdims=True), sink)
        pe = jnp.exp(sb - m)
        den = jnp.sum(pe, axis=-1, keepdims=True) + jnp.exp(sink - m)
        probs.append(pe / den)
    ps = jnp.concatenate(probs, axis=0).astype(BF16)
    o = _dot(ps, vp)
    outs = []
    for gl in range(4):
        o0 = o[(2 * gl) * rows:(2 * gl + 1) * rows]
        o1 = o[(2 * gl + 1) * rows:(2 * gl + 2) * rows]
        outs.append(jnp.where(keeps[gl], o0, pltpu.roll(o1, HEAD_DIM, 1)))
    return jnp.concatenate(outs, axis=1)


def _attn_kernel(*refs, tm, tiles_per_seq, prompt):
    it = iter(refs)
    sink_ref, x_ref, ng_ref, wq_ref, cos_ref, slo_ref, shi_ref, wo_ref = (next(it) for _ in range(8))
    ka_ref, va_ref, kb_ref, vb_ref = (next(it) for _ in range(4))
    out_ref, h_s = next(it), next(it)
    i = pl.program_id(0)
    j = pl.program_id(1)

    @pl.when(j == 0)
    def _():
        x = x_ref[...]
        h_s[...] = _rms(x, ng_ref[...]).astype(BF16)
        out_ref[...] = x

    q = _rope(_dot(h_s[...], wq_ref[...]), cos_ref[...], slo_ref[...], shi_ref[...])
    blk = ATTN_BLOCK
    if prompt:
        first = (i % tiles_per_seq) == 0
        r = lax.broadcasted_iota(jnp.int32, (blk, 2 * blk), 0)
        kk = lax.broadcasted_iota(jnp.int32, (blk, 2 * blk), 1)
        band = (kk > r) & (kk <= r + blk)
        outs = []
        for qb in range(tm // blk):
            cur = slice(qb * blk, (qb + 1) * blk)
            if qb == 0:
                k_prev, v_prev = kb_ref[...], vb_ref[...]
                allowed = band & (kk >= jnp.where(first, blk, 0))
            else:
                prv = slice((qb - 1) * blk, qb * blk)
                k_prev, v_prev = ka_ref[prv, :], va_ref[prv, :]
                allowed = band
            kp = jnp.concatenate([k_prev, ka_ref[cur, :]], axis=0).astype(BF16)
            vp = jnp.concatenate([v_prev, va_ref[cur, :]], axis=0).astype(BF16)
            outs.append(_attend(q[cur], kp, vp, allowed, sink_ref, j))
        o = jnp.concatenate(outs, axis=0)
    else:
        n_groups = tm // GROUP_ROWS
        t = lax.broadcasted_iota(jnp.int32, (GROUP_ROWS, 2 * blk), 0) - STATE_ROWS
        kk = lax.broadcasted_iota(jnp.int32, (GROUP_ROWS, 2 * blk), 1)
        new_t = kk - blk - STATE_ROWS
        allowed = ((kk < blk) & (kk > t)) | ((kk >= blk) & (new_t >= 0) & (new_t <= t) &
                                               (kk < blk + GROUP_ROWS))
        allowed = allowed & (t >= 0)
        pad = jnp.zeros((blk - GROUP_ROWS, LANES), F32)
        outs = []
        for b in range(n_groups):
            rows = slice(b * GROUP_ROWS, (b + 1) * GROUP_ROWS)
            kp = jnp.concatenate([ka_ref[b], kb_ref[rows, :], pad], axis=0).astype(BF16)
            vp = jnp.concatenate([va_ref[b], vb_ref[rows, :], pad], axis=0).astype(BF16)
            outs.append(_attend(q[rows], kp, vp, allowed, sink_ref, j))
        o = jnp.concatenate(outs, axis=0)
    out_ref[...] += _dot(o.astype(BF16), wo_ref[...])


def _attn_call(x, ng, w_q, w_o, sinks, tables, ka, va, kb, vb, *, tm, tiles_per_seq, prompt):
    n_rows = x.shape[0]
    n_tiles = n_rows // tm
    row = lambda i, j: (i, 0)
    fixed = lambda i, j: (0, 0)
    if prompt:
        per = tm // ATTN_BLOCK
        cur_spec = pl.BlockSpec((tm, LANES), lambda i, j: (i, j))
        prev_spec = pl.BlockSpec((ATTN_BLOCK, LANES), lambda i, j: (jnp.maximum(i * per - 1, 0), j))
        kv_specs = [cur_spec, cur_spec, prev_spec, prev_spec]
    else:
        n_groups = tm // GROUP_ROWS
        cache_spec = pl.BlockSpec((n_groups, ATTN_BLOCK, LANES), lambda i, j: (0, 0, j))
        new_spec = pl.BlockSpec((tm, LANES), lambda i, j: (i, j))
        kv_specs = [cache_spec, cache_spec, new_spec, new_spec]
    return pl.pallas_call(
        functools.partial(_attn_kernel, tm=tm, tiles_per_seq=tiles_per_seq, prompt=prompt),
        grid=(n_tiles, N_PAIRS),
        in_specs=[
            pl.BlockSpec(memory_space=pltpu.SMEM),
            pl.BlockSpec((tm, D_MODEL), row),
            pl.BlockSpec((1, D_MODEL), fixed),
            pl.BlockSpec((D_MODEL, PAIR_W), lambda i, j: (0, j)),
            pl.BlockSpec((tm, LANES), row),
            pl.BlockSpec((tm, LANES), row),
            pl.BlockSpec((tm, LANES), row),
            pl.BlockSpec((PAIR_W, D_MODEL), lambda i, j: (j, 0)),
        ] + kv_specs,
        out_specs=pl.BlockSpec((tm, D_MODEL), row),
        out_shape=jax.ShapeDtypeStruct((n_rows, D_MODEL), F32),
        scratch_shapes=[pltpu.VMEM((tm, D_MODEL), BF16)],
        compiler_params=pltpu.CompilerParams(
            dimension_semantics=("arbitrary", "arbitrary"), vmem_limit_bytes=VMEM_LIMIT),
        name="attention_prompt" if prompt else "attention_sample",
    )(sinks, x, ng, w_q, *tables, w_o, ka, va, kb, vb)


def _rope_tables(pos):
    half = HEAD_DIM // 2
    freqs = ROPE_THETA ** (-jnp.arange(half, dtype=F32) / half)
    ang = pos.astype(F32)[:, None] * freqs[None, :]
    cos, sin = jnp.cos(ang), jnp.sin(ang)
    zero = jnp.zeros_like(sin)
    cos_t = jnp.concatenate([cos, cos] * 2, axis=1)
    sin_lo = jnp.concatenate([-sin, zero] * 2, axis=1)
    sin_hi = jnp.concatenate([zero, sin] * 2, axis=1)
    return cos_t, sin_lo, sin_hi


def _to_groups(state, n_state):
    b, _, c = state.shape
    padded = jnp.pad(state, ((0, 0), (STATE_ROWS - n_state, GROUP_ROWS - STATE_ROWS), (0, 0)))
    return padded.reshape(b * GROUP_ROWS, c)


def kernel(x_prompt, x_sample, state_rglru_h, state_rglru_conv, state_ffn_conv, cache_window_k,
           cache_window_v, a_norm, a_w_in, a_conv_w, a_conv_b, a_gate_a_w, a_gate_a_b, a_gate_x_w,
           a_gate_x_b, a_lambda, a_w_out, kv_norm, w_k, w_v, b_norm, w_q, sinks, w_o, f_norm,
           f_w_up, f_conv_w, f_conv_b, f_w_down, final_norm):
    bp, s_len, _ = x_prompt.shape
    bs, t_len, _ = x_sample.shape
    depth = f_norm.shape[0]
    n_a = a_norm.shape[0]
    assert t_len == GROUP_ROWS - STATE_ROWS and cache_window_k.shape[1] == ATTN_BLOCK
    tiles_per_seq = s_len // TM_PROMPT
    ts = bs * GROUP_ROWS

    xp = x_prompt.reshape(bp * s_len, D_MODEL)
    xs = jnp.pad(x_sample, ((0, 0), (STATE_ROWS, 0), (0, 0))).reshape(ts, D_MODEL)
    pos_p = jnp.tile(jnp.arange(s_len, dtype=jnp.int32), bp)
    pos_s = jnp.tile(PAST_LEN - STATE_ROWS + jnp.arange(GROUP_ROWS, dtype=jnp.int32), bs)
    tab_p = _rope_tables(pos_p)
    tab_s = _rope_tables(pos_s)

    bf = lambda w: w.astype(BF16)
    vec = lambda v: v.reshape(1, -1)
    a_w_in_b, a_w_out_b = bf(a_w_in), bf(a_w_out)
    wa_b, wx_b = bf(a_gate_a_w), bf(a_gate_x_w)
    w_kv_b = bf(jnp.concatenate([w_k, w_v], axis=1))
    w_q_b, w_o_b = bf(w_q), bf(w_o)
    f_w_up_b, f_w_down_b = bf(f_w_up), bf(f_w_down)
    cache_k = cache_window_k.reshape(bs, ATTN_BLOCK, KV_LANES)
    cache_v = cache_window_v.reshape(bs, ATTN_BLOCK, KV_LANES)

    h_p, h_s, c_p, c_s, f_p, f_s = [], [], [], [], [], []
    for layer in range(depth):
        if layer < n_a:
            i = layer
            prm = (vec(a_norm[i]), a_w_in_b[i], a_conv_w[i], vec(a_conv_b[i]), wa_b[i],
                   vec(a_gate_a_b[i]), wx_b[i], vec(a_gate_x_b[i]), vec(a_lambda[i]), a_w_out_b[i])
            xp, ut, yt = _rec_call(xp, *prm, None, None, tm=TM_PROMPT, carry=True,
                                   tiles_per_seq=tiles_per_seq)
            last = slice(tiles_per_seq - 1, None, tiles_per_seq)
            c_p.append(ut[last, SUBLANES - 3:])
            h_p.append(yt[last, SUBLANES - 1])
            prev = _to_groups(state_rglru_conv[i], 3)
            h0 = _to_groups(state_rglru_h[i][:, None, :], 1)
            xs, ut, yt = _rec_call(xs, *prm, prev, h0, tm=ts, carry=False, tiles_per_seq=1)
            c_s.append(ut.reshape(bs, GROUP_ROWS, D_RNN)[:, GROUP_ROWS - 3:])
            h_s.append(yt.reshape(bs, GROUP_ROWS, D_RNN)[:, GROUP_ROWS - 1])
        else:
            j = layer - n_a
            if j == 0:
                k_p, v_p = _kv_call(xp, vec(kv_norm), w_kv_b, tab_p, tm=TM_PROMPT)
                k_s, v_s = _kv_call(xs, vec(kv_norm), w_kv_b, tab_s, tm=ts)
            xp = _attn_call(xp, vec(b_norm[j]), w_q_b[j], w_o_b[j], sinks[j], tab_p,
                            k_p, v_p, k_p, v_p, tm=TM_PROMPT, tiles_per_seq=tiles_per_seq,
                            prompt=True)
            xs = _attn_call(xs, vec(b_norm[j]), w_q_b[j], w_o_b[j], sinks[j], tab_s,
                            cache_k, cache_v, k_s, v_s, tm=ts, tiles_per_seq=1, prompt=False)
        fn = vec(final_norm) if layer == depth - 1 else None
        fprm = (vec(f_norm[layer]), f_w_up_b[layer], f_conv_w[layer], vec(f_conv_b[layer]),
                f_w_down_b[layer])
        xp, gt = _ffn_call(xp, *fprm, None, fn, tm=TM_PROMPT, carry=True,
                           tiles_per_seq=tiles_per_seq)
        f_p.append(gt[tiles_per_seq - 1::tiles_per_seq, SUBLANES - 2:])
        prev = _to_groups(state_ffn_conv[layer], 2)
        xs, gt = _ffn_call(xs, *fprm, prev, fn, tm=ts, carry=False, tiles_per_seq=1)
        f_s.append(gt.reshape(bs, GROUP_ROWS, D_FF)[:, GROUP_ROWS - 2:])

    y_prompt = xp.reshape(bp, s_len, D_MODEL)
    y_sample = xs.reshape(bs, GROUP_ROWS, D_MODEL)[:, STATE_ROWS:]
    heads = lambda a: a.reshape(a.shape[0], a.shape[1], N_KV_HEADS, HEAD_DIM)
    win_p = min(ATTN_BLOCK, s_len)
    k_p3 = k_p.reshape(bp, s_len, KV_LANES)
    v_p3 = v_p.reshape(bp, s_len, KV_LANES)
    k_new = k_s.reshape(bs, GROUP_ROWS, KV_LANES)[:, STATE_ROWS:]
    v_new = v_s.reshape(bs, GROUP_ROWS, KV_LANES)[:, STATE_ROWS:]
    k_all = jnp.concatenate([cache_k, k_new], axis=1)
    v_all = jnp.concatenate([cache_v, v_new], axis=1)
    return (y_prompt, y_sample, jnp.stack(h_p), jnp.stack(h_s), jnp.stack(c_p), jnp.stack(c_s),
            jnp.stack(f_p), jnp.stack(f_s), heads(k_p3[:, s_len - win_p:]),
            heads(v_p3[:, s_len - win_p:]), heads(k_all[:, t_len:]), heads(v_all[:, t_len:]))
```

```python
import functools

import jax
import jax.numpy as jnp
from jax import lax
from jax.experimental import pallas as pl
from jax.experimental.pallas import tpu as pltpu

F32 = jnp.float32
BF16 = jnp.bfloat16

D_MODEL = 2048
PAST_LEN = 16384
D_RNN = 2048
RNN_BLOCK = 256
LRU_C = 8.0
N_HEADS = 32
HEAD_DIM = 64
N_KV_HEADS = 8
ATTN_BLOCK = 128
ROPE_THETA = 10000.0
D_FF = 3 * D_MODEL
EPS = 1e-6

LANES = 128
SUBLANES = 8
GROUP_ROWS = SUBLANES
STATE_ROWS = 4
KV_LANES = N_KV_HEADS * HEAD_DIM
PAIR_W = 4 * LANES
N_PAIRS = N_KV_HEADS // 2
VMEM_LIMIT = 56 * 1024 * 1024

TM_PROMPT = 512
SEG = TM_PROMPT // SUBLANES
TC_RNN = 512
TCS_RNN = 256
TF_FFN = 1024
TFS_FFN = 512


def _rms(x, g):
    ms = jnp.mean(x * x, axis=-1, keepdims=True)
    return x * lax.rsqrt(ms + EPS) * g


def _dot(a, b):
    return jnp.dot(a, b, preferred_element_type=F32)


def _shifted(xe, k, drop):
    return pltpu.roll(xe, k, 0)[drop:]


def _rope(x, cos, sin_lo, sin_hi):
    w = x.shape[1]
    n = w // LANES
    cosw = jnp.concatenate([cos] * n, axis=1)
    lo = jnp.concatenate([sin_lo] * n, axis=1)
    hi = jnp.concatenate([sin_hi] * n, axis=1)
    half = HEAD_DIM // 2
    return x * cosw + pltpu.roll(x, w - half, 1) * lo + pltpu.roll(x, half, 1) * hi


def _gate_dots(uc, wa_ref, wx_ref, blocks):
    ucb = uc.astype(BF16)
    ga = jnp.concatenate(
        [_dot(ucb[:, n * RNN_BLOCK:(n + 1) * RNN_BLOCK], wa_ref[bk]) for n, bk in enumerate(blocks)],
        axis=1)
    gx = jnp.concatenate(
        [_dot(ucb[:, n * RNN_BLOCK:(n + 1) * RNN_BLOCK], wx_ref[bk]) for n, bk in enumerate(blocks)],
        axis=1)
    return ga, gx


def _lru_coeffs(ga, gx, ba, bx, lam, uc):
    r = jax.nn.sigmoid(ga + ba)
    ig = jax.nn.sigmoid(gx + bx)
    z = -lam
    softplus = jnp.maximum(z, 0.0) + jnp.log1p(jnp.exp(-jnp.abs(z)))
    log_a = (-LRU_C * r) * softplus
    a = jnp.exp(log_a)
    sq = -jnp.tanh(log_a) * (a * a + 1.0)
    mult = jnp.where(sq > 0.0, sq * lax.rsqrt(sq), 0.0)
    b = mult * ig * uc
    return a, b


def _ffn_kernel(*refs, tm, tf, tfs, n_steps, tiles_per_seq, carry, final):
    it = iter(refs)
    x_ref, ng_ref, wg_ref, wu_ref, cw_ref, cb_ref, wd_ref = (next(it) for _ in range(7))
    prev_ref = None if carry else next(it)
    fn_ref = next(it) if final else None
    out_ref, tail_ref, h_s = next(it), next(it), next(it)
    tail_s = next(it) if carry else None
    i = pl.program_id(0)
    j = pl.program_id(1)

    @pl.when(j == 0)
    def _():
        x = x_ref[...]
        h_s[...] = _rms(x, ng_ref[...]).astype(BF16)
        out_ref[...] = x

    if carry:
        @pl.when((i % tiles_per_seq) == 0)
        def _():
            tail_s[j] = jnp.zeros((SUBLANES, tf), F32)
    else:
        row = lax.broadcasted_iota(jnp.int32, (tm, tfs), 0)
        is_token = (row & (GROUP_ROWS - 1)) >= STATE_ROWS

    h = h_s[...]
    tail_in = tail_s[j] if carry else None
    tails = []
    for sc in range(tf // tfs):
        cols = slice(sc * tfs, (sc + 1) * tfs)
        g = _dot(h, wg_ref[:, cols])
        u = _dot(h, wu_ref[:, cols])
        if carry:
            ge = jnp.concatenate([tail_in[:, cols], g], axis=0)
            drop = SUBLANES
        else:
            g = jnp.where(is_token, g, prev_ref[:, cols])
            ge = g
            drop = 0
        gc = cb_ref[:, cols] + cw_ref[0:1, cols] * _shifted(ge, 2, drop)
        gc = gc + cw_ref[1:2, cols] * _shifted(ge, 1, drop)
        gc = gc + cw_ref[2:3, cols] * g
        act = (jax.nn.gelu(gc) * u).astype(BF16)
        out_ref[...] += _dot(act, wd_ref[cols, :])
        tails.append(g[tm - SUBLANES:] if carry else g)
    tail = jnp.concatenate(tails, axis=1)
    tail_ref[0] = tail
    if carry:
        tail_s[j] = tail
    if final:
        @pl.when(j == n_steps - 1)
        def _():
            out_ref[...] = _rms(out_ref[...], fn_ref[...])


def _ffn_call(x, layer, ng, w_up, cw, cb, w_down, prev, fn, *, tm, carry, tiles_per_seq):
    n_rows = x.shape[0]
    tf, tfs = TF_FFN, TFS_FFN
    n_steps = D_FF // tf
    n_tiles = n_rows // tm
    tail_rows = SUBLANES if carry else tm
    final = fn is not None
    in_specs = [
        pl.BlockSpec((tm, D_MODEL), lambda i, j: (i, 0)),
        pl.BlockSpec((None, 1, D_MODEL), lambda i, j: (layer, 0, 0)),
        pl.BlockSpec((None, D_MODEL, tf), lambda i, j: (layer, 0, j)),
        pl.BlockSpec((None, D_MODEL, tf), lambda i, j: (layer, 0, j + n_steps)),
        pl.BlockSpec((None, 3, tf), lambda i, j: (layer, 0, j)),
        pl.BlockSpec((None, 1, tf), lambda i, j: (layer, 0, j)),
        pl.BlockSpec((None, tf, D_MODEL), lambda i, j: (layer, j, 0)),
    ]
    args = [x, ng, w_up, w_up, cw, cb, w_down]
    if not carry:
        in_specs.append(pl.BlockSpec((tm, tf), lambda i, j: (i, j)))
        args.append(prev)
    if final:
        in_specs.append(pl.BlockSpec((1, D_MODEL), lambda i, j: (0, 0)))
        args.append(fn)
    scratch = [pltpu.VMEM((tm, D_MODEL), BF16)]
    if carry:
        scratch.append(pltpu.VMEM((n_steps, SUBLANES, tf), F32))
    return pl.pallas_call(
        functools.partial(_ffn_kernel, tm=tm, tf=tf, tfs=tfs, n_steps=n_steps,
                          tiles_per_seq=tiles_per_seq, carry=carry, final=final),
        grid=(n_tiles, n_steps),
        in_specs=in_specs,
        out_specs=[
            pl.BlockSpec((tm, D_MODEL), lambda i, j: (i, 0)),
            pl.BlockSpec((1, tail_rows, tf), lambda i, j: (i, 0, j)),
        ],
        out_shape=[
            jax.ShapeDtypeStruct((n_rows, D_MODEL), F32),
            jax.ShapeDtypeStruct((n_tiles, tail_rows, D_FF), F32),
        ],
        scratch_shapes=scratch,
        compiler_params=pltpu.CompilerParams(
            dimension_semantics=("arbitrary", "arbitrary"), vmem_limit_bytes=VMEM_LIMIT),
        name="conv_ffn_prompt" if carry else "conv_ffn_sample",
    )(*args)


def _rec_in_specs(layer, tm, tc, n_steps):
    nb = tc // RNN_BLOCK
    chunk = lambda i, j: (layer, 0, j)
    return [
        pl.BlockSpec((tm, D_MODEL), lambda i, j: (i, 0)),
        pl.BlockSpec((None, 1, D_MODEL), lambda i, j: (layer, 0, 0)),
        pl.BlockSpec((None, D_MODEL, tc), chunk),
        pl.BlockSpec((None, D_MODEL, tc), lambda i, j: (layer, 0, j + n_steps)),
        pl.BlockSpec((None, 4, tc), chunk),
        pl.BlockSpec((None, 1, tc), chunk),
        pl.BlockSpec((None, nb, RNN_BLOCK, RNN_BLOCK), lambda i, j: (layer, j, 0, 0)),
        pl.BlockSpec((None, 1, tc), chunk),
        pl.BlockSpec((None, nb, RNN_BLOCK, RNN_BLOCK), lambda i, j: (layer, j, 0, 0)),
        pl.BlockSpec((None, 1, tc), chunk),
        pl.BlockSpec((None, 1, tc), chunk),
        pl.BlockSpec((None, tc, D_MODEL), lambda i, j: (layer, j, 0)),
    ]


def _rec_prompt_kernel(x_ref, ng_ref, wgt_ref, wu_ref, cw_ref, cb_ref, wa_ref, ba_ref, wx_ref,
                       bx_ref, lam_ref, wo_ref, out_ref, utail_ref, hlast_ref,
                       h_s, acc_s, utail_s, hc_s, *, tm, tc, tcs, n_steps, tiles_per_seq):
    seg = tm // SUBLANES
    i = pl.program_id(0)
    j = pl.program_id(1)

    @pl.when(j == 0)
    def _():
        x = jnp.swapaxes(x_ref[...].reshape(SUBLANES, seg, D_MODEL), 0, 1)
        x = x.reshape(tm, D_MODEL)
        h_s[...] = _rms(x, ng_ref[...]).astype(BF16)
        acc_s[...] = x

    @pl.when((i % tiles_per_seq) == 0)
    def _():
        utail_s[j] = jnp.zeros((SUBLANES, tc), F32)
        hc_s[j] = jnp.zeros((SUBLANES, tc), F32)

    h = h_s[...]
    sub0 = lax.broadcasted_iota(jnp.int32, (SUBLANES, tcs), 0) == 0
    piece = lambda v, k: v[k * SUBLANES:(k + 1) * SUBLANES]
    tail_in = utail_s[j]
    hc_in = hc_s[j]
    sub_cols = [slice(sc * tcs, (sc + 1) * tcs) for sc in range(tc // tcs)]
    u_tails, h_lasts = [], []
    u_ahead = _dot(h, wu_ref[:, sub_cols[0]])
    for sc, cols in enumerate(sub_cols):
        u = u_ahead
        up = [piece(u, k) for k in range(seg)]
        tail = tail_in[:, cols]
        before = {}
        for d in (1, 2, 3):
            rolled = pltpu.roll(up[seg - d], 1, 0)
            before[-d] = jnp.where(sub0, tail[SUBLANES - d:SUBLANES - d + 1], rolled)
        at = lambda k: up[k] if k >= 0 else before[k]
        cw = cw_ref[:, cols]
        cb = cb_ref[:, cols]
        ucs = []
        for k in range(seg):
            v = cb + cw[0:1] * at(k - 3)
            v = v + cw[1:2] * at(k - 2)
            v = v + cw[2:3] * at(k - 1)
            ucs.append(v + cw[3:4] * up[k])
        uc = jnp.concatenate(ucs, axis=0)
        blocks = [sc * (tcs // RNN_BLOCK) + n for n in range(tcs // RNN_BLOCK)]
        ga, gx = _gate_dots(uc, wa_ref, wx_ref, blocks)
        gate = _dot(h, wgt_ref[:, cols])
        if sc + 1 < len(sub_cols):
            u_ahead = _dot(h, wu_ref[:, sub_cols[sc + 1]])
        a, b = _lru_coeffs(ga, gx, ba_ref[:, cols], bx_ref[:, cols], lam_ref[:, cols], uc)
        a_cum, b_cum = [piece(a, 0)], [piece(b, 0)]
        for k in range(1, seg):
            ak = piece(a, k)
            b_cum.append(ak * b_cum[-1] + piece(b, k))
            a_cum.append(ak * a_cum[-1])
        a_seg, b_seg = a_cum[-1], b_cum[-1]
        c = hc_in[0:1, cols]
        carries = [c]
        for s in range(SUBLANES):
            c = a_seg[s:s + 1] * c + b_seg[s:s + 1]
            carries.append(c)
        carry_in = jnp.concatenate(carries[:SUBLANES], axis=0)
        y = jnp.concatenate([a_cum[k] * carry_in + b_cum[k] for k in range(seg)], axis=0)
        act = (jax.nn.gelu(gate) * y).astype(BF16)
        acc_s[...] += _dot(act, wo_ref[cols, :])
        h_lasts.append(jnp.broadcast_to(carries[SUBLANES], (SUBLANES, tcs)))
        last = SUBLANES - 1
        u_tails.append(jnp.concatenate(
            [jnp.zeros((SUBLANES - 3, tcs), F32)] + [up[seg - d][last:last + 1] for d in (3, 2, 1)],
            axis=0))
    h_last = jnp.concatenate(h_lasts, axis=1)
    u_tail = jnp.concatenate(u_tails, axis=1)
    hc_s[j] = h_last
    hlast_ref[0] = h_last
    utail_s[j] = u_tail
    utail_ref[0] = u_tail

    @pl.when(j == n_steps - 1)
    def _():
        acc = jnp.swapaxes(acc_s[...].reshape(seg, SUBLANES, D_MODEL), 0, 1)
        out_ref[...] = acc.reshape(tm, D_MODEL)


def _rec_prompt_call(x, layer, prm, *, tm, tiles_per_seq):
    n_rows = x.shape[0]
    tc, tcs = TC_RNN, TCS_RNN
    n_steps = D_RNN // tc
    n_tiles = n_rows // tm
    tail_spec = pl.BlockSpec((1, SUBLANES, tc), lambda i, j: (i, 0, j))
    tail_shape = jax.ShapeDtypeStruct((n_tiles, SUBLANES, D_RNN), F32)
    return pl.pallas_call(
        functools.partial(_rec_prompt_kernel, tm=tm, tc=tc, tcs=tcs, n_steps=n_steps,
                          tiles_per_seq=tiles_per_seq),
        grid=(n_tiles, n_steps),
        in_specs=_rec_in_specs(layer, tm, tc, n_steps),
        out_specs=[pl.BlockSpec((tm, D_MODEL), lambda i, j: (i, 0)), tail_spec, tail_spec],
        out_shape=[jax.ShapeDtypeStruct((n_rows, D_MODEL), F32), tail_shape, tail_shape],
        scratch_shapes=[pltpu.VMEM((tm, D_MODEL), BF16), pltpu.VMEM((tm, D_MODEL), F32),
                        pltpu.VMEM((n_steps, SUBLANES, tc), F32),
                        pltpu.VMEM((n_steps, SUBLANES, tc), F32)],
        compiler_params=pltpu.CompilerParams(
            dimension_semantics=("arbitrary", "arbitrary"), vmem_limit_bytes=VMEM_LIMIT),
        name="recurrent_prompt",
    )(x, *prm)


def _rec_sample_kernel(x_ref, ng_ref, wgt_ref, wu_ref, cw_ref, cb_ref, wa_ref, ba_ref, wx_ref,
                       bx_ref, lam_ref, wo_ref, prev_ref, h0_ref, out_ref, utail_ref, ytail_ref,
                       h_s, *, tm, tc):
    j = pl.program_id(1)

    @pl.when(j == 0)
    def _():
        x = x_ref[...]
        h_s[...] = _rms(x, ng_ref[...]).astype(BF16)
        out_ref[...] = x

    h = h_s[...]
    gate = _dot(h, wgt_ref[...])
    u = _dot(h, wu_ref[...])
    row8 = lax.broadcasted_iota(jnp.int32, (tm, tc), 0) & (GROUP_ROWS - 1)
    is_token = row8 >= STATE_ROWS
    u = jnp.where(is_token, u, prev_ref[...])
    cw = cw_ref[...]
    uc = cb_ref[...] + cw[0:1] * _shifted(u, 3, 0)
    uc = uc + cw[1:2] * _shifted(u, 2, 0)
    uc = uc + cw[2:3] * _shifted(u, 1, 0)
    uc = uc + cw[3:4] * u
    ga, gx = _gate_dots(uc, wa_ref, wx_ref, list(range(tc // RNN_BLOCK)))
    a, b = _lru_coeffs(ga, gx, ba_ref[...], bx_ref[...], lam_ref[...], uc)
    a = jnp.where(is_token, a, 0.0)
    b = jnp.where(is_token, b, h0_ref[...])
    for d in (1, 2, 4):
        m = row8 >= d
        a_sh = pltpu.roll(a, d, 0)
        b_sh = pltpu.roll(b, d, 0)
        b = jnp.where(m, a * b_sh + b, b)
        a = jnp.where(m, a * a_sh, a)
    act = (jax.nn.gelu(gate) * b).astype(BF16)
    out_ref[...] += _dot(act, wo_ref[...])
    utail_ref[...] = u
    ytail_ref[...] = b


def _rec_sample_call(x, layer, prm, prev, h0, *, tm):
    n_rows = x.shape[0]
    tc = TC_RNN
    n_steps = D_RNN // tc
    state_spec = pl.BlockSpec((tm, tc), lambda i, j: (i, j))
    state_shape = jax.ShapeDtypeStruct((n_rows, D_RNN), F32)
    return pl.pallas_call(
        functools.partial(_rec_sample_kernel, tm=tm, tc=tc),
        grid=(n_rows // tm, n_steps),
        in_specs=_rec_in_specs(layer, tm, tc, n_steps) + [state_spec, state_spec],
        out_specs=[pl.BlockSpec((tm, D_MODEL), lambda i, j: (i, 0)), state_spec, state_spec],
        out_shape=[jax.ShapeDtypeStruct((n_rows, D_MODEL), F32), state_shape, state_shape],
        scratch_shapes=[pltpu.VMEM((tm, D_MODEL), BF16)],
        compiler_params=pltpu.CompilerParams(
            dimension_semantics=("arbitrary", "arbitrary"), vmem_limit_bytes=VMEM_LIMIT),
        name="recurrent_sample",
    )(x, *prm, prev, h0)


def _kv_kernel(x_ref, ng_ref, w_ref, cos_ref, slo_ref, shi_ref, k_ref, v_ref):
    h = _rms(x_ref[...], ng_ref[...]).astype(BF16)
    kv = _dot(h, w_ref[...])
    k_ref[...] = _rope(kv[:, :KV_LANES], cos_ref[...], slo_ref[...], shi_ref[...])
    v_ref[...] = kv[:, KV_LANES:]


def _kv_call(x, ng, w_kv, tables, *, tm):
    n_rows = x.shape[0]
    row = lambda i: (i, 0)
    fixed = lambda i: (0, 0)
    return pl.pallas_call(
        _kv_kernel,
        grid=(n_rows // tm,),
        in_specs=[
            pl.BlockSpec((tm, D_MODEL), row),
            pl.BlockSpec((1, D_MODEL), fixed),
            pl.BlockSpec((D_MODEL, 2 * KV_LANES), fixed),
            pl.BlockSpec((tm, LANES), row),
            pl.BlockSpec((tm, LANES), row),
            pl.BlockSpec((tm, LANES), row),
        ],
        out_specs=[pl.BlockSpec((tm, KV_LANES), row), pl.BlockSpec((tm, KV_LANES), row)],
        out_shape=[jax.ShapeDtypeStruct((n_rows, KV_LANES), F32)] * 2,
        compiler_params=pltpu.CompilerParams(
            dimension_semantics=("arbitrary",), vmem_limit_bytes=VMEM_LIMIT),
        name="shared_kv",
    )(x, ng, w_kv, *tables)


def _scores(q, kp, pair):
    rows = q.shape[0]
    hi_half = lax.broadcasted_iota(jnp.int32, (rows, LANES), 1) >= HEAD_DIM
    blocks, heads, keeps = [], [], []
    for gl in range(4):
        p = gl // 2
        keep = hi_half if p else jnp.logical_not(hi_half)
        qg = q[:, gl * LANES:(gl + 1) * LANES]
        blocks.append(jnp.where(keep, qg, 0.0))
        blocks.append(jnp.where(keep, pltpu.roll(qg, HEAD_DIM, 1), 0.0))
        g = 4 * pair + gl
        heads += [2 * g + p, 2 * g + 1 - p]
        keeps.append(keep)
    qs = jnp.concatenate(blocks, axis=0).astype(BF16)
    s = lax.dot_general(qs, kp, (((1,), (1,)), ((), ())), preferred_element_type=F32)
    return s * (HEAD_DIM ** -0.5), heads, keeps


def _attend(scored, vp, allowed, sink_ref, layer):
    s, heads, keeps = scored
    rows = s.shape[0] // 8
    probs = []
    for bi in range(8):
        sb = jnp.where(allowed, s[bi * rows:(bi + 1) * rows], -jnp.inf)
        sink = sink_ref[layer, heads[bi]]
        m = jnp.maximum(jnp.max(sb, axis=-1, keepdims=True), sink)
        pe = jnp.exp(sb - m)
        den = jnp.sum(pe, axis=-1, keepdims=True) + jnp.exp(sink - m)
        probs.append(pe / den)
    ps = jnp.concatenate(probs, axis=0).astype(BF16)
    o = _dot(ps, vp)
    outs = []
    for gl in range(4):
        o0 = o[(2 * gl) * rows:(2 * gl + 1) * rows]
        o1 = o[(2 * gl + 1) * rows:(2 * gl + 2) * rows]
        outs.append(jnp.where(keeps[gl], o0, pltpu.roll(o1, HEAD_DIM, 1)))
    return jnp.concatenate(outs, axis=1)


def _attn_kernel(*refs, layer, tm, tiles_per_seq, prompt):
    it = iter(refs)
    sink_ref, x_ref, ng_ref, wq_ref, cos_ref, slo_ref, shi_ref, wo_ref = (next(it) for _ in range(8))
    ka_ref, va_ref, kb_ref, vb_ref = (next(it) for _ in range(4))
    out_ref, h_s = next(it), next(it)
    i = pl.program_id(0)
    j = pl.program_id(1)

    @pl.when(j == 0)
    def _():
        x = x_ref[...]
        h_s[...] = _rms(x, ng_ref[...]).astype(BF16)
        out_ref[...] = x

    q = _rope(_dot(h_s[...], wq_ref[...]), cos_ref[...], slo_ref[...], shi_ref[...])
    blk = ATTN_BLOCK
    if prompt:
        first = (i % tiles_per_seq) == 0
        r = lax.broadcasted_iota(jnp.int32, (blk, 2 * blk), 0)
        kk = lax.broadcasted_iota(jnp.int32, (blk, 2 * blk), 1)
        band = (kk > r) & (kk <= r + blk)
        scored, values, masks = [], [], []
        for qb in range(tm // blk):
            cur = slice(qb * blk, (qb + 1) * blk)
            if qb == 0:
                k_prev, v_prev = kb_ref[...], vb_ref[...]
                masks.append(band & (kk >= jnp.where(first, blk, 0)))
            else:
                prv = slice((qb - 1) * blk, qb * blk)
                k_prev, v_prev = ka_ref[prv, :], va_ref[prv, :]
                masks.append(band)
            kp = jnp.concatenate([k_prev, ka_ref[cur, :]], axis=0).astype(BF16)
            values.append(jnp.concatenate([v_prev, va_ref[cur, :]], axis=0).astype(BF16))
            scored.append(_scores(q[cur], kp, j))
        o = jnp.concatenate(
            [_attend(sc, vp, allowed, sink_ref, layer)
             for sc, vp, allowed in zip(scored, values, masks)], axis=0)
    else:
        n_groups = tm // GROUP_ROWS
        t = lax.broadcasted_iota(jnp.int32, (GROUP_ROWS, 2 * blk), 0) - STATE_ROWS
        kk = lax.broadcasted_iota(jnp.int32, (GROUP_ROWS, 2 * blk), 1)
        new_t = kk - blk - STATE_ROWS
        allowed = ((kk < blk) & (kk > t)) | ((kk >= blk) & (new_t >= 0) & (new_t <= t) &
                                               (kk < blk + GROUP_ROWS))
        allowed = allowed & (t >= 0)
        pad = jnp.zeros((blk - GROUP_ROWS, LANES), F32)
        scored, values = [], []
        for b in range(n_groups):
            rows = slice(b * GROUP_ROWS, (b + 1) * GROUP_ROWS)
            kp = jnp.concatenate([ka_ref[b], kb_ref[rows, :], pad], axis=0).astype(BF16)
            values.append(jnp.concatenate([va_ref[b], vb_ref[rows, :], pad], axis=0).astype(BF16))
            scored.append(_scores(q[rows], kp, j))
        o = jnp.concatenate(
            [_attend(sc, vp, allowed, sink_ref, layer) for sc, vp in zip(scored, values)], axis=0)
    out_ref[...] += _dot(o.astype(BF16), wo_ref[...])


def _attn_call(x, layer, ng, w_q, w_o, sinks, tables, ka, va, kb, vb, *, tm, tiles_per_seq, prompt):
    n_rows = x.shape[0]
    n_tiles = n_rows // tm
    row = lambda i, j: (i, 0)
    if prompt:
        per = tm // ATTN_BLOCK
        cur_spec = pl.BlockSpec((tm, LANES), lambda i, j: (i, j))
        prev_spec = pl.BlockSpec((ATTN_BLOCK, LANES), lambda i, j: (jnp.maximum(i * per - 1, 0), j))
        kv_specs = [cur_spec, cur_spec, prev_spec, prev_spec]
    else:
        n_groups = tm // GROUP_ROWS
        cache_spec = pl.BlockSpec((n_groups, ATTN_BLOCK, LANES), lambda i, j: (0, 0, j))
        new_spec = pl.BlockSpec((tm, LANES), lambda i, j: (i, j))
        kv_specs = [cache_spec, cache_spec, new_spec, new_spec]
    return pl.pallas_call(
        functools.partial(_attn_kernel, layer=layer, tm=tm, tiles_per_seq=tiles_per_seq,
                          prompt=prompt),
        grid=(n_tiles, N_PAIRS),
        in_specs=[
            pl.BlockSpec(memory_space=pltpu.SMEM),
            pl.BlockSpec((tm, D_MODEL), row),
            pl.BlockSpec((None, 1, D_MODEL), lambda i, j: (layer, 0, 0)),
            pl.BlockSpec((None, D_MODEL, PAIR_W), lambda i, j: (layer, 0, j)),
            pl.BlockSpec((tm, LANES), row),
            pl.BlockSpec((tm, LANES), row),
            pl.BlockSpec((tm, LANES), row),
            pl.BlockSpec((None, PAIR_W, D_MODEL), lambda i, j: (layer, j, 0)),
        ] + kv_specs,
        out_specs=pl.BlockSpec((tm, D_MODEL), row),
        out_shape=jax.ShapeDtypeStruct((n_rows, D_MODEL), F32),
        scratch_shapes=[pltpu.VMEM((tm, D_MODEL), BF16)],
        compiler_params=pltpu.CompilerParams(
            dimension_semantics=("arbitrary", "arbitrary"), vmem_limit_bytes=VMEM_LIMIT),
        name="attention_prompt" if prompt else "attention_sample",
    )(sinks, x, ng, w_q, *tables, w_o, ka, va, kb, vb)


def _rope_tables(pos):
    half = HEAD_DIM // 2
    freqs = ROPE_THETA ** (-jnp.arange(half, dtype=F32) / half)
    ang = pos.astype(F32)[:, None] * freqs[None, :]
    cos, sin = jnp.cos(ang), jnp.sin(ang)
    zero = jnp.zeros_like(sin)
    cos_t = jnp.concatenate([cos, cos] * 2, axis=1)
    sin_lo = jnp.concatenate([-sin, zero] * 2, axis=1)
    sin_hi = jnp.concatenate([zero, sin] * 2, axis=1)
    return cos_t, sin_lo, sin_hi


def _to_groups(state, n_state):
    n, b, _, c = state.shape
    padded = jnp.pad(
        state, ((0, 0), (0, 0), (STATE_ROWS - n_state, GROUP_ROWS - STATE_ROWS), (0, 0)))
    return padded.reshape(n, b * GROUP_ROWS, c)


def kernel(x_prompt, x_sample, state_rglru_h, state_rglru_conv, state_ffn_conv, cache_window_k,
           cache_window_v, a_norm, a_w_in, a_conv_w, a_conv_b, a_gate_a_w, a_gate_a_b, a_gate_x_w,
           a_gate_x_b, a_lambda, a_w_out, kv_norm, w_k, w_v, b_norm, w_q, sinks, w_o, f_norm,
           f_w_up, f_conv_w, f_conv_b, f_w_down, final_norm):
    bp, s_len, _ = x_prompt.shape
    bs, t_len, _ = x_sample.shape
    depth = f_norm.shape[0]
    n_a = a_norm.shape[0]
    assert t_len == GROUP_ROWS - STATE_ROWS and cache_window_k.shape[1] == ATTN_BLOCK
    tiles_per_seq = s_len // TM_PROMPT
    ts = bs * GROUP_ROWS

    xp = x_prompt.reshape(bp * s_len, D_MODEL)
    xs = jnp.pad(x_sample, ((0, 0), (STATE_ROWS, 0), (0, 0))).reshape(ts, D_MODEL)
    pos_p = jnp.tile(jnp.arange(s_len, dtype=jnp.int32), bp)
    pos_s = jnp.tile(PAST_LEN - STATE_ROWS + jnp.arange(GROUP_ROWS, dtype=jnp.int32), bs)
    tab_p = _rope_tables(pos_p)
    tab_s = _rope_tables(pos_s)

    bf = lambda w: w.astype(BF16)
    vec = lambda v: v.reshape(v.shape[:-1] + (1, v.shape[-1]))
    a_w_in_b = bf(a_w_in)
    a_prm = (vec(a_norm), a_w_in_b, a_w_in_b, a_conv_w, vec(a_conv_b), bf(a_gate_a_w),
             vec(a_gate_a_b), bf(a_gate_x_w), vec(a_gate_x_b), vec(a_lambda), bf(a_w_out))
    f_prm = (vec(f_norm), bf(f_w_up), f_conv_w, vec(f_conv_b), bf(f_w_down))
    w_kv_b = bf(jnp.concatenate([w_k, w_v], axis=1))
    w_q_b, w_o_b, b_norm3 = bf(w_q), bf(w_o), vec(b_norm)
    cache_k = cache_window_k.reshape(bs, ATTN_BLOCK, KV_LANES)
    cache_v = cache_window_v.reshape(bs, ATTN_BLOCK, KV_LANES)
    rec_prev = _to_groups(state_rglru_conv, 3)
    rec_h0 = _to_groups(state_rglru_h[:, :, None, :], 1)
    ffn_prev = _to_groups(state_ffn_conv, 2)
    last = slice(tiles_per_seq - 1, None, tiles_per_seq)

    h_p, h_s, c_p, c_s, f_p, f_s = [], [], [], [], [], []
    for layer in range(depth):
        if layer < n_a:
            xp, ut, ht = _rec_prompt_call(xp, layer, a_prm, tm=TM_PROMPT,
                                          tiles_per_seq=tiles_per_seq)
            c_p.append(ut[last, SUBLANES - 3:])
            h_p.append(ht[last, 0])
            xs, ut, yt = _rec_sample_call(xs, layer, a_prm, rec_prev[layer], rec_h0[layer], tm=ts)
            c_s.append(ut.reshape(bs, GROUP_ROWS, D_RNN)[:, GROUP_ROWS - 3:])
            h_s.append(yt.reshape(bs, GROUP_ROWS, D_RNN)[:, GROUP_ROWS - 1])
        else:
            j = layer - n_a
            if j == 0:
                k_p, v_p = _kv_call(xp, vec(kv_norm), w_kv_b, tab_p, tm=TM_PROMPT)
                k_s, v_s = _kv_call(xs, vec(kv_norm), w_kv_b, tab_s, tm=ts)
            xp = _attn_call(xp, j, b_norm3, w_q_b, w_o_b, sinks, tab_p, k_p, v_p, k_p, v_p,
                            tm=TM_PROMPT, tiles_per_seq=tiles_per_seq, prompt=True)
            xs = _attn_call(xs, j, b_norm3, w_q_b, w_o_b, sinks, tab_s, cache_k, cache_v, k_s, v_s,
                            tm=ts, tiles_per_seq=1, prompt=False)
        fn = vec(final_norm) if layer == depth - 1 else None
        xp, gt = _ffn_call(xp, layer, *f_prm, None, fn, tm=TM_PROMPT, carry=True,
                           tiles_per_seq=tiles_per_seq)
        f_p.append(gt[last, SUBLANES - 2:])
        xs, gt = _ffn_call(xs, layer, *f_prm, ffn_prev[layer], fn, tm=ts, carry=False,
                           tiles_per_seq=1)
        f_s.append(gt.reshape(bs, GROUP_ROWS, D_FF)[:, GROUP_ROWS - 2:])

    y_prompt = xp.reshape(bp, s_len, D_MODEL)
    y_sample = xs.reshape(bs, GROUP_ROWS, D_MODEL)[:, STATE_ROWS:]
    heads = lambda a: a.reshape(a.shape[0], a.shape[1], N_KV_HEADS, HEAD_DIM)
    win_p = min(ATTN_BLOCK, s_len)
    k_p3 = k_p.reshape(bp, s_len, KV_LANES)
    v_p3 = v_p.reshape(bp, s_len, KV_LANES)
    k_new = k_s.reshape(bs, GROUP_ROWS, KV_LANES)[:, STATE_ROWS:]
    v_new = v_s.reshape(bs, GROUP_ROWS, KV_LANES)[:, STATE_ROWS:]
    k_all = jnp.concatenate([cache_k, k_new], axis=1)
    v_all = jnp.concatenate([cache_v, v_new], axis=1)
    return (y_prompt, y_sample, jnp.stack(h_p), jnp.stack(h_s), jnp.stack(c_p), jnp.stack(c_s),
            jnp.stack(f_p), jnp.stack(f_s), heads(k_p3[:, s_len - win_p:]),
            heads(v_p3[:, s_len - win_p:]), heads(k_all[:, t_len:]), heads(v_all[:, t_len:]))
```

```python
import functools

import jax
import jax.numpy as jnp
from jax import lax
from jax.experimental import pallas as pl
from jax.experimental.pallas import tpu as pltpu

F32 = jnp.float32
BF16 = jnp.bfloat16

D_MODEL = 2048
PAST_LEN = 16384
D_RNN = 2048
RNN_BLOCK = 256
LRU_C = 8.0
N_HEADS = 32
HEAD_DIM = 64
N_KV_HEADS = 8
ATTN_BLOCK = 128
ROPE_THETA = 10000.0
D_FF = 3 * D_MODEL
EPS = 1e-6

LANES = 128
SUBLANES = 8
GROUP_ROWS = SUBLANES
STATE_ROWS = 4
KV_LANES = N_KV_HEADS * HEAD_DIM
PAIR_W = 4 * LANES
N_PAIRS = N_KV_HEADS // 2
VMEM_LIMIT = 56 * 1024 * 1024

TM_PROMPT = 512
SEG = TM_PROMPT // SUBLANES
TC_RNN = 512
TCS_RNN = 256
TC_RNN_SAMPLE = 256
TF_FFN = 1024
TFS_FFN = 512
TF_FFN_SAMPLE = 512


def _rms(x, g):
    ms = jnp.mean(x * x, axis=-1, keepdims=True)
    return x * lax.rsqrt(ms + EPS) * g


def _dot(a, b):
    return jnp.dot(a, b, preferred_element_type=F32)


def _shifted(xe, k, drop):
    return pltpu.roll(xe, k, 0)[drop:]


def _rope(x, cos, sin_lo, sin_hi):
    w = x.shape[1]
    n = w // LANES
    cosw = jnp.concatenate([cos] * n, axis=1)
    lo = jnp.concatenate([sin_lo] * n, axis=1)
    hi = jnp.concatenate([sin_hi] * n, axis=1)
    half = HEAD_DIM // 2
    return x * cosw + pltpu.roll(x, w - half, 1) * lo + pltpu.roll(x, half, 1) * hi


def _gate_dots(uc, wa_ref, wx_ref, blocks):
    ucb = uc.astype(BF16)
    ga = jnp.concatenate(
        [_dot(ucb[:, n * RNN_BLOCK:(n + 1) * RNN_BLOCK], wa_ref[bk]) for n, bk in enumerate(blocks)],
        axis=1)
    gx = jnp.concatenate(
        [_dot(ucb[:, n * RNN_BLOCK:(n + 1) * RNN_BLOCK], wx_ref[bk]) for n, bk in enumerate(blocks)],
        axis=1)
    return ga, gx


def _lru_coeffs(ga, gx, ba, bx, lam, uc):
    r = jax.nn.sigmoid(ga + ba)
    ig = jax.nn.sigmoid(gx + bx)
    z = -lam
    softplus = jnp.maximum(z, 0.0) + jnp.log1p(jnp.exp(-jnp.abs(z)))
    log_a = (-LRU_C * r) * softplus
    a = jnp.exp(log_a)
    sq = -jnp.tanh(log_a) * (a * a + 1.0)
    mult = jnp.where(sq > 0.0, sq * lax.rsqrt(sq), 0.0)
    b = mult * ig * uc
    return a, b


def _ffn_kernel(*refs, tm, tf, tfs, n_steps, tiles_per_seq, carry, final):
    it = iter(refs)
    x_ref, ng_ref, wg_ref, wu_ref, cw_ref, cb_ref, wd_ref = (next(it) for _ in range(7))
    prev_ref = None if carry else next(it)
    fn_ref = next(it) if final else None
    out_ref, tail_ref = next(it), next(it)
    w_out_refs = None if carry else (next(it), next(it), next(it))
    h_s = next(it)
    tail_s = next(it) if carry else None
    i = pl.program_id(0)
    j = pl.program_id(1)

    @pl.when(j == 0)
    def _():
        x = x_ref[...]
        h_s[...] = _rms(x, ng_ref[...]).astype(BF16)
        out_ref[...] = x

    if carry:
        @pl.when((i % tiles_per_seq) == 0)
        def _():
            tail_s[j] = jnp.zeros((SUBLANES, tf), F32)
        wg, wu, wd = wg_ref, wu_ref, wd_ref
    else:
        row = lax.broadcasted_iota(jnp.int32, (tm, tfs), 0)
        is_token = (row & (GROUP_ROWS - 1)) >= STATE_ROWS
        wg, wu, wd = (w[...].astype(BF16) for w in (wg_ref, wu_ref, wd_ref))
        for o_ref, w in zip(w_out_refs, (wg, wu, wd)):
            o_ref[...] = w

    h = h_s[...]
    tail_in = tail_s[j] if carry else None
    tails = []
    for sc in range(tf // tfs):
        cols = slice(sc * tfs, (sc + 1) * tfs)
        g = _dot(h, wg[:, cols])
        u = _dot(h, wu[:, cols])
        if carry:
            ge = jnp.concatenate([tail_in[:, cols], g], axis=0)
            drop = SUBLANES
        else:
            g = jnp.where(is_token, g, prev_ref[:, cols])
            ge = g
            drop = 0
        gc = cb_ref[:, cols] + cw_ref[0:1, cols] * _shifted(ge, 2, drop)
        gc = gc + cw_ref[1:2, cols] * _shifted(ge, 1, drop)
        gc = gc + cw_ref[2:3, cols] * g
        act = (jax.nn.gelu(gc) * u).astype(BF16)
        out_ref[...] += _dot(act, wd[cols, :])
        tails.append(g[tm - SUBLANES:] if carry else g)
    tail = jnp.concatenate(tails, axis=1)
    tail_ref[0] = tail
    if carry:
        tail_s[j] = tail
    if final:
        @pl.when(j == n_steps - 1)
        def _():
            out_ref[...] = _rms(out_ref[...], fn_ref[...])


def _ffn_call(x, layer, ng, wg, wu, cw, cb, wd, prev, fn, *, tm, carry, tiles_per_seq):
    n_rows = x.shape[0]
    tf, tfs = (TF_FFN, TFS_FFN) if carry else (TF_FFN_SAMPLE, TF_FFN_SAMPLE)
    n_steps = D_FF // tf
    n_tiles = n_rows // tm
    tail_rows = SUBLANES if carry else tm
    final = fn is not None
    if carry:
        w_specs = [pl.BlockSpec((D_MODEL, tf), lambda i, j: (0, j)),
                   pl.BlockSpec((D_MODEL, tf), lambda i, j: (0, j)),
                   pl.BlockSpec((tf, D_MODEL), lambda i, j: (j, 0))]
    else:
        w_specs = [pl.BlockSpec((None, D_MODEL, tf), lambda i, j: (layer, 0, j)),
                   pl.BlockSpec((None, D_MODEL, tf), lambda i, j: (layer, 0, j + n_steps)),
                   pl.BlockSpec((None, tf, D_MODEL), lambda i, j: (layer, j, 0))]
    in_specs = [
        pl.BlockSpec((tm, D_MODEL), lambda i, j: (i, 0)),
        pl.BlockSpec((None, 1, D_MODEL), lambda i, j: (layer, 0, 0)),
        w_specs[0],
        w_specs[1],
        pl.BlockSpec((None, 3, tf), lambda i, j: (layer, 0, j)),
        pl.BlockSpec((None, 1, tf), lambda i, j: (layer, 0, j)),
        w_specs[2],
    ]
    args = [x, ng, wg, wu, cw, cb, wd]
    out_specs = [
        pl.BlockSpec((tm, D_MODEL), lambda i, j: (i, 0)),
        pl.BlockSpec((1, tail_rows, tf), lambda i, j: (i, 0, j)),
    ]
    out_shape = [
        jax.ShapeDtypeStruct((n_rows, D_MODEL), F32),
        jax.ShapeDtypeStruct((n_tiles, tail_rows, D_FF), F32),
    ]
    if not carry:
        in_specs.append(pl.BlockSpec((tm, tf), lambda i, j: (i, j)))
        args.append(prev)
        out_specs += [pl.BlockSpec((D_MODEL, tf), lambda i, j: (0, j)),
                      pl.BlockSpec((D_MODEL, tf), lambda i, j: (0, j)),
                      pl.BlockSpec((tf, D_MODEL), lambda i, j: (j, 0))]
        out_shape += [jax.ShapeDtypeStruct((D_MODEL, D_FF), BF16),
                      jax.ShapeDtypeStruct((D_MODEL, D_FF), BF16),
                      jax.ShapeDtypeStruct((D_FF, D_MODEL), BF16)]
    if final:
        in_specs.append(pl.BlockSpec((1, D_MODEL), lambda i, j: (0, 0)))
        args.append(fn)
    scratch = [pltpu.VMEM((tm, D_MODEL), BF16)]
    if carry:
        scratch.append(pltpu.VMEM((n_steps, SUBLANES, tf), F32))
    return pl.pallas_call(
        functools.partial(_ffn_kernel, tm=tm, tf=tf, tfs=tfs, n_steps=n_steps,
                          tiles_per_seq=tiles_per_seq, carry=carry, final=final),
        grid=(n_tiles, n_steps),
        in_specs=in_specs,
        out_specs=out_specs,
        out_shape=out_shape,
        scratch_shapes=scratch,
        compiler_params=pltpu.CompilerParams(
            dimension_semantics=("arbitrary", "arbitrary"), vmem_limit_bytes=VMEM_LIMIT),
        name="conv_ffn_prompt" if carry else "conv_ffn_sample",
    )(*args)


def _rec_weight_specs(tc):
    nb = tc // RNN_BLOCK
    return [
        pl.BlockSpec((D_MODEL, tc), lambda i, j: (0, j)),
        pl.BlockSpec((D_MODEL, tc), lambda i, j: (0, j)),
        pl.BlockSpec((nb, RNN_BLOCK, RNN_BLOCK), lambda i, j: (j, 0, 0)),
        pl.BlockSpec((nb, RNN_BLOCK, RNN_BLOCK), lambda i, j: (j, 0, 0)),
        pl.BlockSpec((tc, D_MODEL), lambda i, j: (j, 0)),
    ]


def _rec_in_specs(layer, tm, tc, n_steps, stacked):
    nb = tc // RNN_BLOCK
    chunk = lambda i, j: (layer, 0, j)
    if stacked:
        w_gate, w_u, w_a, w_x, w_out = [
            pl.BlockSpec((None, D_MODEL, tc), chunk),
            pl.BlockSpec((None, D_MODEL, tc), lambda i, j: (layer, 0, j + n_steps)),
            pl.BlockSpec((None, nb, RNN_BLOCK, RNN_BLOCK), lambda i, j: (layer, j, 0, 0)),
            pl.BlockSpec((None, nb, RNN_BLOCK, RNN_BLOCK), lambda i, j: (layer, j, 0, 0)),
            pl.BlockSpec((None, tc, D_MODEL), lambda i, j: (layer, j, 0)),
        ]
    else:
        w_gate, w_u, w_a, w_x, w_out = _rec_weight_specs(tc)
    return [
        pl.BlockSpec((tm, D_MODEL), lambda i, j: (i, 0)),
        pl.BlockSpec((None, 1, D_MODEL), lambda i, j: (layer, 0, 0)),
        w_gate,
        w_u,
        pl.BlockSpec((None, 4, tc), chunk),
        pl.BlockSpec((None, 1, tc), chunk),
        w_a,
        pl.BlockSpec((None, 1, tc), chunk),
        w_x,
        pl.BlockSpec((None, 1, tc), chunk),
        pl.BlockSpec((None, 1, tc), chunk),
        w_out,
    ]


def _rec_prompt_kernel(x_ref, ng_ref, wgt_ref, wu_ref, cw_ref, cb_ref, wa_ref, ba_ref, wx_ref,
                       bx_ref, lam_ref, wo_ref, out_ref, utail_ref, hlast_ref,
                       h_s, acc_s, utail_s, hc_s, *, tm, tc, tcs, n_steps, tiles_per_seq):
    seg = tm // SUBLANES
    i = pl.program_id(0)
    j = pl.program_id(1)

    @pl.when(j == 0)
    def _():
        x = jnp.swapaxes(x_ref[...].reshape(SUBLANES, seg, D_MODEL), 0, 1)
        x = x.reshape(tm, D_MODEL)
        h_s[...] = _rms(x, ng_ref[...]).astype(BF16)
        acc_s[...] = x

    @pl.when((i % tiles_per_seq) == 0)
    def _():
        utail_s[j] = jnp.zeros((SUBLANES, tc), F32)
        hc_s[j] = jnp.zeros((SUBLANES, tc), F32)

    h = h_s[...]
    sub0 = lax.broadcasted_iota(jnp.int32, (SUBLANES, tcs), 0) == 0
    piece = lambda v, k: v[k * SUBLANES:(k + 1) * SUBLANES]
    tail_in = utail_s[j]
    hc_in = hc_s[j]
    sub_cols = [slice(sc * tcs, (sc + 1) * tcs) for sc in range(tc // tcs)]
    u_tails, h_lasts = [], []
    u_ahead = _dot(h, wu_ref[:, sub_cols[0]])
    for sc, cols in enumerate(sub_cols):
        u = u_ahead
        up = [piece(u, k) for k in range(seg)]
        tail = tail_in[:, cols]
        before = {}
        for d in (1, 2, 3):
            rolled = pltpu.roll(up[seg - d], 1, 0)
            before[-d] = jnp.where(sub0, tail[SUBLANES - d:SUBLANES - d + 1], rolled)
        at = lambda k: up[k] if k >= 0 else before[k]
        cw = cw_ref[:, cols]
        cb = cb_ref[:, cols]
        ucs = []
        for k in range(seg):
            v = cb + cw[0:1] * at(k - 3)
            v = v + cw[1:2] * at(k - 2)
            v = v + cw[2:3] * at(k - 1)
            ucs.append(v + cw[3:4] * up[k])
        uc = jnp.concatenate(ucs, axis=0)
        blocks = [sc * (tcs // RNN_BLOCK) + n for n in range(tcs // RNN_BLOCK)]
        ga, gx = _gate_dots(uc, wa_ref, wx_ref, blocks)
        gate = _dot(h, wgt_ref[:, cols])
        if sc + 1 < len(sub_cols):
            u_ahead = _dot(h, wu_ref[:, sub_cols[sc + 1]])
        a, b = _lru_coeffs(ga, gx, ba_ref[:, cols], bx_ref[:, cols], lam_ref[:, cols], uc)
        a_cum, b_cum = [piece(a, 0)], [piece(b, 0)]
        for k in range(1, seg):
            ak = piece(a, k)
            b_cum.append(ak * b_cum[-1] + piece(b, k))
            a_cum.append(ak * a_cum[-1])
        a_seg, b_seg = a_cum[-1], b_cum[-1]
        c = hc_in[0:1, cols]
        carries = [c]
        for s in range(SUBLANES):
            c = a_seg[s:s + 1] * c + b_seg[s:s + 1]
            carries.append(c)
        carry_in = jnp.concatenate(carries[:SUBLANES], axis=0)
        y = jnp.concatenate([a_cum[k] * carry_in + b_cum[k] for k in range(seg)], axis=0)
        act = (jax.nn.gelu(gate) * y).astype(BF16)
        acc_s[...] += _dot(act, wo_ref[cols, :])
        h_lasts.append(jnp.broadcast_to(carries[SUBLANES], (SUBLANES, tcs)))
        last = SUBLANES - 1
        u_tails.append(jnp.concatenate(
            [jnp.zeros((SUBLANES - 3, tcs), F32)] + [up[seg - d][last:last + 1] for d in (3, 2, 1)],
            axis=0))
    h_last = jnp.concatenate(h_lasts, axis=1)
    u_tail = jnp.concatenate(u_tails, axis=1)
    hc_s[j] = h_last
    hlast_ref[0] = h_last
    utail_s[j] = u_tail
    utail_ref[0] = u_tail

    @pl.when(j == n_steps - 1)
    def _():
        acc = jnp.swapaxes(acc_s[...].reshape(seg, SUBLANES, D_MODEL), 0, 1)
        out_ref[...] = acc.reshape(tm, D_MODEL)


def _rec_prompt_call(x, layer, prm, *, tm, tiles_per_seq):
    n_rows = x.shape[0]
    tc, tcs = TC_RNN, TCS_RNN
    n_steps = D_RNN // tc
    n_tiles = n_rows // tm
    tail_spec = pl.BlockSpec((1, SUBLANES, tc), lambda i, j: (i, 0, j))
    tail_shape = jax.ShapeDtypeStruct((n_tiles, SUBLANES, D_RNN), F32)
    return pl.pallas_call(
        functools.partial(_rec_prompt_kernel, tm=tm, tc=tc, tcs=tcs, n_steps=n_steps,
                          tiles_per_seq=tiles_per_seq),
        grid=(n_tiles, n_steps),
        in_specs=_rec_in_specs(layer, tm, tc, n_steps, stacked=False),
        out_specs=[pl.BlockSpec((tm, D_MODEL), lambda i, j: (i, 0)), tail_spec, tail_spec],
        out_shape=[jax.ShapeDtypeStruct((n_rows, D_MODEL), F32), tail_shape, tail_shape],
        scratch_shapes=[pltpu.VMEM((tm, D_MODEL), BF16), pltpu.VMEM((tm, D_MODEL), F32),
                        pltpu.VMEM((n_steps, SUBLANES, tc), F32),
                        pltpu.VMEM((n_steps, SUBLANES, tc), F32)],
        compiler_params=pltpu.CompilerParams(
            dimension_semantics=("arbitrary", "arbitrary"), vmem_limit_bytes=VMEM_LIMIT),
        name="recurrent_prompt",
    )(x, *prm)


def _rec_sample_kernel(x_ref, ng_ref, wgt_ref, wu_ref, cw_ref, cb_ref, wa_ref, ba_ref, wx_ref,
                       bx_ref, lam_ref, wo_ref, prev_ref, h0_ref, out_ref, utail_ref, ytail_ref,
                       wgt_o, wu_o, wa_o, wx_o, wo_o, h_s, *, tm, tc):
    j = pl.program_id(1)
    wgt, wu, wa, wx, wo = (w[...].astype(BF16) for w in (wgt_ref, wu_ref, wa_ref, wx_ref, wo_ref))
    for o_ref, w in zip((wgt_o, wu_o, wa_o, wx_o, wo_o), (wgt, wu, wa, wx, wo)):
        o_ref[...] = w

    @pl.when(j == 0)
    def _():
        x = x_ref[...]
        h_s[...] = _rms(x, ng_ref[...]).astype(BF16)
        out_ref[...] = x

    h = h_s[...]
    gate = _dot(h, wgt)
    u = _dot(h, wu)
    row8 = lax.broadcasted_iota(jnp.int32, (tm, tc), 0) & (GROUP_ROWS - 1)
    is_token = row8 >= STATE_ROWS
    u = jnp.where(is_token, u, prev_ref[...])
    cw = cw_ref[...]
    uc = cb_ref[...] + cw[0:1] * _shifted(u, 3, 0)
    uc = uc + cw[1:2] * _shifted(u, 2, 0)
    uc = uc + cw[2:3] * _shifted(u, 1, 0)
    uc = uc + cw[3:4] * u
    ga, gx = _gate_dots(uc, wa, wx, list(range(tc // RNN_BLOCK)))
    a, b = _lru_coeffs(ga, gx, ba_ref[...], bx_ref[...], lam_ref[...], uc)
    a = jnp.where(is_token, a, 0.0)
    b = jnp.where(is_token, b, h0_ref[...])
    for d in (1, 2, 4):
        m = row8 >= d
        a_sh = pltpu.roll(a, d, 0)
        b_sh = pltpu.roll(b, d, 0)
        b = jnp.where(m, a * b_sh + b, b)
        a = jnp.where(m, a * a_sh, a)
    act = (jax.nn.gelu(gate) * b).astype(BF16)
    out_ref[...] += _dot(act, wo)
    utail_ref[...] = u
    ytail_ref[...] = b


def _rec_sample_call(x, layer, prm, prev, h0, *, tm):
    n_rows = x.shape[0]
    tc = TC_RNN_SAMPLE
    n_steps = D_RNN // tc
    state_spec = pl.BlockSpec((tm, tc), lambda i, j: (i, j))
    state_shape = jax.ShapeDtypeStruct((n_rows, D_RNN), F32)
    return pl.pallas_call(
        functools.partial(_rec_sample_kernel, tm=tm, tc=tc),
        grid=(n_rows // tm, n_steps),
        in_specs=_rec_in_specs(layer, tm, tc, n_steps, stacked=True) + [state_spec, state_spec],
        out_specs=[pl.BlockSpec((tm, D_MODEL), lambda i, j: (i, 0)), state_spec, state_spec]
        + _rec_weight_specs(tc),
        out_shape=[jax.ShapeDtypeStruct((n_rows, D_MODEL), F32), state_shape, state_shape,
                   jax.ShapeDtypeStruct((D_MODEL, D_RNN), BF16),
                   jax.ShapeDtypeStruct((D_MODEL, D_RNN), BF16),
                   jax.ShapeDtypeStruct((D_RNN // RNN_BLOCK, RNN_BLOCK, RNN_BLOCK), BF16),
                   jax.ShapeDtypeStruct((D_RNN // RNN_BLOCK, RNN_BLOCK, RNN_BLOCK), BF16),
                   jax.ShapeDtypeStruct((D_RNN, D_MODEL), BF16)],
        scratch_shapes=[pltpu.VMEM((tm, D_MODEL), BF16)],
        compiler_params=pltpu.CompilerParams(
            dimension_semantics=("arbitrary", "arbitrary"), vmem_limit_bytes=VMEM_LIMIT),
        name="recurrent_sample",
    )(x, *prm, prev, h0)


def _kv_kernel(x_ref, ng_ref, wk_ref, wv_ref, cos_ref, slo_ref, shi_ref, k_ref, v_ref):
    h = _rms(x_ref[...], ng_ref[...]).astype(BF16)
    k = _dot(h, wk_ref[...].astype(BF16))
    k_ref[...] = _rope(k, cos_ref[...], slo_ref[...], shi_ref[...])
    v_ref[...] = _dot(h, wv_ref[...].astype(BF16))


def _kv_call(x, ng, w_k, w_v, tables, *, tm):
    n_rows = x.shape[0]
    row = lambda i: (i, 0)
    fixed = lambda i: (0, 0)
    return pl.pallas_call(
        _kv_kernel,
        grid=(n_rows // tm,),
        in_specs=[
            pl.BlockSpec((tm, D_MODEL), row),
            pl.BlockSpec((1, D_MODEL), fixed),
            pl.BlockSpec((D_MODEL, KV_LANES), fixed),
            pl.BlockSpec((D_MODEL, KV_LANES), fixed),
            pl.BlockSpec((tm, LANES), row),
            pl.BlockSpec((tm, LANES), row),
            pl.BlockSpec((tm, LANES), row),
        ],
        out_specs=[pl.BlockSpec((tm, KV_LANES), row), pl.BlockSpec((tm, KV_LANES), row)],
        out_shape=[jax.ShapeDtypeStruct((n_rows, KV_LANES), F32)] * 2,
        compiler_params=pltpu.CompilerParams(
            dimension_semantics=("arbitrary",), vmem_limit_bytes=VMEM_LIMIT),
        name="shared_kv",
    )(x, ng, w_k, w_v, *tables)


def _scores(q, kp, pair):
    rows = q.shape[0]
    hi_half = lax.broadcasted_iota(jnp.int32, (rows, LANES), 1) >= HEAD_DIM
    blocks, heads, keeps = [], [], []
    for gl in range(4):
        p = gl // 2
        keep = hi_half if p else jnp.logical_not(hi_half)
        qg = q[:, gl * LANES:(gl + 1) * LANES]
        blocks.append(jnp.where(keep, qg, 0.0))
        blocks.append(jnp.where(keep, pltpu.roll(qg, HEAD_DIM, 1), 0.0))
        g = 4 * pair + gl
        heads += [2 * g + p, 2 * g + 1 - p]
        keeps.append(keep)
    qs = jnp.concatenate(blocks, axis=0).astype(BF16)
    s = lax.dot_general(qs, kp, (((1,), (1,)), ((), ())), preferred_element_type=F32)
    return s * (HEAD_DIM ** -0.5), heads, keeps


def _attend(scored, vp, allowed, sink_ref, layer):
    s, heads, keeps = scored
    rows = s.shape[0] // 8
    probs = []
    for bi in range(8):
        sb = jnp.where(allowed, s[bi * rows:(bi + 1) * rows], -jnp.inf)
        sink = sink_ref[layer, heads[bi]]
        m = jnp.maximum(jnp.max(sb, axis=-1, keepdims=True), sink)
        pe = jnp.exp(sb - m)
        den = jnp.sum(pe, axis=-1, keepdims=True) + jnp.exp(sink - m)
        probs.append(pe / den)
    ps = jnp.concatenate(probs, axis=0).astype(BF16)
    o = _dot(ps, vp)
    outs = []
    for gl in range(4):
        o0 = o[(2 * gl) * rows:(2 * gl + 1) * rows]
        o1 = o[(2 * gl + 1) * rows:(2 * gl + 2) * rows]
        outs.append(jnp.where(keeps[gl], o0, pltpu.roll(o1, HEAD_DIM, 1)))
    return jnp.concatenate(outs, axis=1)


def _attn_kernel(*refs, layer, tm, tiles_per_seq, prompt):
    it = iter(refs)
    sink_ref, x_ref, ng_ref, wq_ref, cos_ref, slo_ref, shi_ref, wo_ref = (next(it) for _ in range(8))
    ka_ref, va_ref, kb_ref, vb_ref = (next(it) for _ in range(4))
    out_ref = next(it)
    wq_o, wo_o = (None, None) if prompt else (next(it), next(it))
    h_s = next(it)
    i = pl.program_id(0)
    j = pl.program_id(1)
    if prompt:
        wq, wo = wq_ref[...], wo_ref[...]
    else:
        wq, wo = wq_ref[...].astype(BF16), wo_ref[...].astype(BF16)
        wq_o[...] = wq
        wo_o[...] = wo

    @pl.when(j == 0)
    def _():
        x = x_ref[...]
        h_s[...] = _rms(x, ng_ref[...]).astype(BF16)
        out_ref[...] = x

    q = _rope(_dot(h_s[...], wq), cos_ref[...], slo_ref[...], shi_ref[...])
    blk = ATTN_BLOCK
    if prompt:
        first = (i % tiles_per_seq) == 0
        r = lax.broadcasted_iota(jnp.int32, (blk, 2 * blk), 0)
        kk = lax.broadcasted_iota(jnp.int32, (blk, 2 * blk), 1)
        band = (kk > r) & (kk <= r + blk)
        scored, values, masks = [], [], []
        for qb in range(tm // blk):
            cur = slice(qb * blk, (qb + 1) * blk)
            if qb == 0:
                k_prev, v_prev = kb_ref[...], vb_ref[...]
                masks.append(band & (kk >= jnp.where(first, blk, 0)))
            else:
                prv = slice((qb - 1) * blk, qb * blk)
                k_prev, v_prev = ka_ref[prv, :], va_ref[prv, :]
                masks.append(band)
            kp = jnp.concatenate([k_prev, ka_ref[cur, :]], axis=0).astype(BF16)
            values.append(jnp.concatenate([v_prev, va_ref[cur, :]], axis=0).astype(BF16))
            scored.append(_scores(q[cur], kp, j))
        o = jnp.concatenate(
            [_attend(sc, vp, allowed, sink_ref, layer)
             for sc, vp, allowed in zip(scored, values, masks)], axis=0)
    else:
        n_groups = tm // GROUP_ROWS
        t = lax.broadcasted_iota(jnp.int32, (GROUP_ROWS, 2 * blk), 0) - STATE_ROWS
        kk = lax.broadcasted_iota(jnp.int32, (GROUP_ROWS, 2 * blk), 1)
        new_t = kk - blk - STATE_ROWS
        allowed = ((kk < blk) & (kk > t)) | ((kk >= blk) & (new_t >= 0) & (new_t <= t) &
                                               (kk < blk + GROUP_ROWS))
        allowed = allowed & (t >= 0)
        pad = jnp.zeros((blk - GROUP_ROWS, LANES), F32)
        scored, values = [], []
        for b in range(n_groups):
            rows = slice(b * GROUP_ROWS, (b + 1) * GROUP_ROWS)
            kp = jnp.concatenate([ka_ref[b], kb_ref[rows, :], pad], axis=0).astype(BF16)
            values.append(jnp.concatenate([va_ref[b], vb_ref[rows, :], pad], axis=0).astype(BF16))
            scored.append(_scores(q[rows], kp, j))
        o = jnp.concatenate(
            [_attend(sc, vp, allowed, sink_ref, layer) for sc, vp in zip(scored, values)], axis=0)
    out_ref[...] += _dot(o.astype(BF16), wo)


def _attn_call(x, layer, ng, w_q, w_o, sinks, tables, ka, va, kb, vb, *, tm, tiles_per_seq, prompt):
    n_rows = x.shape[0]
    n_tiles = n_rows // tm
    row = lambda i, j: (i, 0)
    if prompt:
        per = tm // ATTN_BLOCK
        cur_spec = pl.BlockSpec((tm, LANES), lambda i, j: (i, j))
        prev_spec = pl.BlockSpec((ATTN_BLOCK, LANES), lambda i, j: (jnp.maximum(i * per - 1, 0), j))
        kv_specs = [cur_spec, cur_spec, prev_spec, prev_spec]
    else:
        n_groups = tm // GROUP_ROWS
        cache_spec = pl.BlockSpec((n_groups, ATTN_BLOCK, LANES), lambda i, j: (0, 0, j))
        new_spec = pl.BlockSpec((tm, LANES), lambda i, j: (i, j))
        kv_specs = [cache_spec, cache_spec, new_spec, new_spec]
    wq_copy = pl.BlockSpec((D_MODEL, PAIR_W), lambda i, j: (0, j))
    wo_copy = pl.BlockSpec((PAIR_W, D_MODEL), lambda i, j: (j, 0))
    x_shape = jax.ShapeDtypeStruct((n_rows, D_MODEL), F32)
    if prompt:
        wq_spec, wo_spec = wq_copy, wo_copy
        out_specs, out_shape = pl.BlockSpec((tm, D_MODEL), row), x_shape
    else:
        wq_spec = pl.BlockSpec((None, D_MODEL, PAIR_W), lambda i, j: (layer, 0, j))
        wo_spec = pl.BlockSpec((None, PAIR_W, D_MODEL), lambda i, j: (layer, j, 0))
        out_specs = [pl.BlockSpec((tm, D_MODEL), row), wq_copy, wo_copy]
        out_shape = [x_shape, jax.ShapeDtypeStruct((D_MODEL, N_HEADS * HEAD_DIM), BF16),
                     jax.ShapeDtypeStruct((N_HEADS * HEAD_DIM, D_MODEL), BF16)]
    return pl.pallas_call(
        functools.partial(_attn_kernel, layer=layer, tm=tm, tiles_per_seq=tiles_per_seq,
                          prompt=prompt),
        grid=(n_tiles, N_PAIRS),
        in_specs=[
            pl.BlockSpec(memory_space=pltpu.SMEM),
            pl.BlockSpec((tm, D_MODEL), row),
            pl.BlockSpec((None, 1, D_MODEL), lambda i, j: (layer, 0, 0)),
            wq_spec,
            pl.BlockSpec((tm, LANES), row),
            pl.BlockSpec((tm, LANES), row),
            pl.BlockSpec((tm, LANES), row),
            wo_spec,
        ] + kv_specs,
        out_specs=out_specs,
        out_shape=out_shape,
        scratch_shapes=[pltpu.VMEM((tm, D_MODEL), BF16)],
        compiler_params=pltpu.CompilerParams(
            dimension_semantics=("arbitrary", "arbitrary"), vmem_limit_bytes=VMEM_LIMIT),
        name="attention_prompt" if prompt else "attention_sample",
    )(sinks, x, ng, w_q, *tables, w_o, ka, va, kb, vb)


def _rope_tables(pos):
    half = HEAD_DIM // 2
    freqs = ROPE_THETA ** (-jnp.arange(half, dtype=F32) / half)
    ang = pos.astype(F32)[:, None] * freqs[None, :]
    cos, sin = jnp.cos(ang), jnp.sin(ang)
    zero = jnp.zeros_like(sin)
    cos_t = jnp.concatenate([cos, cos] * 2, axis=1)
    sin_lo = jnp.concatenate([-sin, zero] * 2, axis=1)
    sin_hi = jnp.concatenate([zero, sin] * 2, axis=1)
    return cos_t, sin_lo, sin_hi


def _to_groups(state, n_state):
    n, b, _, c = state.shape
    padded = jnp.pad(
        state, ((0, 0), (0, 0), (STATE_ROWS - n_state, GROUP_ROWS - STATE_ROWS), (0, 0)))
    return padded.reshape(n, b * GROUP_ROWS, c)


def kernel(x_prompt, x_sample, state_rglru_h, state_rglru_conv, state_ffn_conv, cache_window_k,
           cache_window_v, a_norm, a_w_in, a_conv_w, a_conv_b, a_gate_a_w, a_gate_a_b, a_gate_x_w,
           a_gate_x_b, a_lambda, a_w_out, kv_norm, w_k, w_v, b_norm, w_q, sinks, w_o, f_norm,
           f_w_up, f_conv_w, f_conv_b, f_w_down, final_norm):
    bp, s_len, _ = x_prompt.shape
    bs, t_len, _ = x_sample.shape
    depth = f_norm.shape[0]
    n_a = a_norm.shape[0]
    assert t_len == GROUP_ROWS - STATE_ROWS and cache_window_k.shape[1] == ATTN_BLOCK
    tiles_per_seq = s_len // TM_PROMPT
    ts = bs * GROUP_ROWS

    xp = x_prompt.reshape(bp * s_len, D_MODEL)
    xs = jnp.pad(x_sample, ((0, 0), (STATE_ROWS, 0), (0, 0))).reshape(ts, D_MODEL)
    pos_p = jnp.tile(jnp.arange(s_len, dtype=jnp.int32), bp)
    pos_s = jnp.tile(PAST_LEN - STATE_ROWS + jnp.arange(GROUP_ROWS, dtype=jnp.int32), bs)
    tab_p = _rope_tables(pos_p)
    tab_s = _rope_tables(pos_s)

    vec = lambda v: v.reshape(v.shape[:-1] + (1, v.shape[-1]))
    a_norm3, a_conv_b3, a_lambda3 = vec(a_norm), vec(a_conv_b), vec(a_lambda)
    a_gate_a_b3, a_gate_x_b3 = vec(a_gate_a_b), vec(a_gate_x_b)
    f_norm3, f_conv_b3, b_norm3 = vec(f_norm), vec(f_conv_b), vec(b_norm)

    def rec_params(w_gate_in, w_u_in, w_a, w_x, w_out):
        return (a_norm3, w_gate_in, w_u_in, a_conv_w, a_conv_b3, w_a, a_gate_a_b3, w_x,
                a_gate_x_b3, a_lambda3, w_out)
    cache_k = cache_window_k.reshape(bs, ATTN_BLOCK, KV_LANES)
    cache_v = cache_window_v.reshape(bs, ATTN_BLOCK, KV_LANES)
    rec_prev = _to_groups(state_rglru_conv, 3)
    rec_h0 = _to_groups(state_rglru_h[:, :, None, :], 1)
    ffn_prev = _to_groups(state_ffn_conv, 2)
    last = slice(tiles_per_seq - 1, None, tiles_per_seq)

    h_p, h_s, c_p, c_s, f_p, f_s = [], [], [], [], [], []
    for layer in range(depth):
        if layer < n_a:
            prm = rec_params(a_w_in, a_w_in, a_gate_a_w, a_gate_x_w, a_w_out)
            xs, ut, yt, *w_b = _rec_sample_call(xs, layer, prm, rec_prev[layer], rec_h0[layer],
                                                tm=ts)
            c_s.append(ut.reshape(bs, GROUP_ROWS, D_RNN)[:, GROUP_ROWS - 3:])
            h_s.append(yt.reshape(bs, GROUP_ROWS, D_RNN)[:, GROUP_ROWS - 1])
            xp, ut, ht = _rec_prompt_call(xp, layer, rec_params(*w_b), tm=TM_PROMPT,
                                          tiles_per_seq=tiles_per_seq)
            c_p.append(ut[last, SUBLANES - 3:])
            h_p.append(ht[last, 0])
        else:
            j = layer - n_a
            if j == 0:
                k_s, v_s = _kv_call(xs, vec(kv_norm), w_k, w_v, tab_s, tm=ts)
                k_p, v_p = _kv_call(xp, vec(kv_norm), w_k, w_v, tab_p, tm=TM_PROMPT)
            xs, w_q_b, w_o_b = _attn_call(xs, j, b_norm3, w_q, w_o, sinks, tab_s, cache_k, cache_v,
                                          k_s, v_s, tm=ts, tiles_per_seq=1, prompt=False)
            xp = _attn_call(xp, j, b_norm3, w_q_b, w_o_b, sinks, tab_p, k_p, v_p, k_p, v_p,
                            tm=TM_PROMPT, tiles_per_seq=tiles_per_seq, prompt=True)
        fn = vec(final_norm) if layer == depth - 1 else None
        xs, gt, wg_b, wu_b, wd_b = _ffn_call(
            xs, layer, f_norm3, f_w_up, f_w_up, f_conv_w, f_conv_b3, f_w_down, ffn_prev[layer], fn,
            tm=ts, carry=False, tiles_per_seq=1)
        f_s.append(gt.reshape(bs, GROUP_ROWS, D_FF)[:, GROUP_ROWS - 2:])
        xp, gt = _ffn_call(xp, layer, f_norm3, wg_b, wu_b, f_conv_w, f_conv_b3, wd_b, None, fn,
                           tm=TM_PROMPT, carry=True, tiles_per_seq=tiles_per_seq)
        f_p.append(gt[last, SUBLANES - 2:])

    y_prompt = xp.reshape(bp, s_len, D_MODEL)
    y_sample = xs.reshape(bs, GROUP_ROWS, D_MODEL)[:, STATE_ROWS:]
    heads = lambda a: a.reshape(a.shape[0], a.shape[1], N_KV_HEADS, HEAD_DIM)
    win_p = min(ATTN_BLOCK, s_len)
    k_p3 = k_p.reshape(bp, s_len, KV_LANES)
    v_p3 = v_p.reshape(bp, s_len, KV_LANES)
    k_new = k_s.reshape(bs, GROUP_ROWS, KV_LANES)[:, STATE_ROWS:]
    v_new = v_s.reshape(bs, GROUP_ROWS, KV_LANES)[:, STATE_ROWS:]
    k_all = jnp.concatenate([cache_k, k_new], axis=1)
    v_all = jnp.concatenate([cache_v, v_new], axis=1)
    return (y_prompt, y_sample, jnp.stack(h_p), jnp.stack(h_s), jnp.stack(c_p), jnp.stack(c_s),
            jnp.stack(f_p), jnp.stack(f_s), heads(k_p3[:, s_len - win_p:]),
            heads(v_p3[:, s_len - win_p:]), heads(k_all[:, t_len:]), heads(v_all[:, t_len:]))
```

```python
import functools

import jax
import jax.numpy as jnp
from jax import lax
from jax.experimental import pallas as pl
from jax.experimental.pallas import tpu as pltpu

F32 = jnp.float32
BF16 = jnp.bfloat16

D_MODEL = 2048
PAST_LEN = 16384
D_RNN = 2048
RNN_BLOCK = 256
LRU_C = 8.0
N_HEADS = 32
HEAD_DIM = 64
N_KV_HEADS = 8
ATTN_BLOCK = 128
ROPE_THETA = 10000.0
D_FF = 3 * D_MODEL
EPS = 1e-6

LANES = 128
SUBLANES = 8
GROUP_ROWS = SUBLANES
STATE_ROWS = 4
KV_LANES = N_KV_HEADS * HEAD_DIM
PAIR_W = 4 * LANES
N_PAIRS = N_KV_HEADS // 2
PAIRS_PER_STEP = 2
VMEM_LIMIT = 56 * 1024 * 1024

TM_PROMPT = 512
SEG = TM_PROMPT // SUBLANES
TC_RNN = 512
TCS_RNN = 256
TC_RNN_SAMPLE = 256
TF_FFN = 1024
TFS_FFN = 512
TF_FFN_SAMPLE = 512


def _rms(x, g):
    ms = jnp.mean(x * x, axis=-1, keepdims=True)
    return x * lax.rsqrt(ms + EPS) * g


def _dot(a, b):
    return jnp.dot(a, b, preferred_element_type=F32)


def _shifted(xe, k, drop):
    return pltpu.roll(xe, k, 0)[drop:]


def _rope(x, cos, sin_lo, sin_hi):
    w = x.shape[1]
    n = w // LANES
    cosw = jnp.concatenate([cos] * n, axis=1)
    lo = jnp.concatenate([sin_lo] * n, axis=1)
    hi = jnp.concatenate([sin_hi] * n, axis=1)
    half = HEAD_DIM // 2
    return x * cosw + pltpu.roll(x, w - half, 1) * lo + pltpu.roll(x, half, 1) * hi


def _gate_dots(uc, wa_ref, wx_ref, blocks):
    ucb = uc.astype(BF16)
    ga = jnp.concatenate(
        [_dot(ucb[:, n * RNN_BLOCK:(n + 1) * RNN_BLOCK], wa_ref[bk]) for n, bk in enumerate(blocks)],
        axis=1)
    gx = jnp.concatenate(
        [_dot(ucb[:, n * RNN_BLOCK:(n + 1) * RNN_BLOCK], wx_ref[bk]) for n, bk in enumerate(blocks)],
        axis=1)
    return ga, gx


def _lru_coeffs(ga, gx, ba, bx, lam, uc):
    r = jax.nn.sigmoid(ga + ba)
    ig = jax.nn.sigmoid(gx + bx)
    z = -lam
    softplus = jnp.maximum(z, 0.0) + jnp.log1p(jnp.exp(-jnp.abs(z)))
    log_a = (-LRU_C * r) * softplus
    a = jnp.exp(log_a)
    sq = -jnp.tanh(log_a) * (a * a + 1.0)
    mult = jnp.where(sq > 0.0, sq * lax.rsqrt(sq), 0.0)
    b = mult * ig * uc
    return a, b


def _ffn_kernel(*refs, tm, tf, tfs, n_steps, tiles_per_seq, carry, final, cast_own, cast_next):
    it = iter(refs)
    x_ref, ng_ref, wg_ref, wu_ref, cw_ref, cb_ref, wd_ref = (next(it) for _ in range(7))
    prev_ref = None if carry else next(it)
    fn_ref = next(it) if final else None
    next_src = (next(it), next(it)) if cast_next else ()
    out_ref, tail_ref = next(it), next(it)
    w_out_refs = (next(it), next(it), next(it)) if cast_own else None
    next_dst = (next(it), next(it)) if cast_next else ()
    h_s = next(it)
    tail_s = next(it) if carry else None
    i = pl.program_id(0)
    j = pl.program_id(1)

    @pl.when(j == 0)
    def _():
        x = x_ref[...]
        h_s[...] = _rms(x, ng_ref[...]).astype(BF16)
        out_ref[...] = x

    if carry:
        @pl.when((i % tiles_per_seq) == 0)
        def _():
            tail_s[j] = jnp.zeros((SUBLANES, tf), F32)
    else:
        row = lax.broadcasted_iota(jnp.int32, (tm, tfs), 0)
        is_token = (row & (GROUP_ROWS - 1)) >= STATE_ROWS
    wg, wu, wd = wg_ref, wu_ref, wd_ref
    if cast_own:
        wg, wu, wd = (w[...].astype(BF16) for w in (wg_ref, wu_ref, wd_ref))
        for o_ref, w in zip(w_out_refs, (wg, wu, wd)):
            o_ref[...] = w
    for src, dst in zip(next_src, next_dst):
        dst[...] = src[...].astype(BF16)

    h = h_s[...]
    tail_in = tail_s[j] if carry else None
    tails = []
    for sc in range(tf // tfs):
        cols = slice(sc * tfs, (sc + 1) * tfs)
        g = _dot(h, wg[:, cols])
        u = _dot(h, wu[:, cols])
        if carry:
            ge = jnp.concatenate([tail_in[:, cols], g], axis=0)
            drop = SUBLANES
        else:
            g = jnp.where(is_token, g, prev_ref[:, cols])
            ge = g
            drop = 0
        gc = cb_ref[:, cols] + cw_ref[0:1, cols] * _shifted(ge, 2, drop)
        gc = gc + cw_ref[1:2, cols] * _shifted(ge, 1, drop)
        gc = gc + cw_ref[2:3, cols] * g
        act = (jax.nn.gelu(gc) * u).astype(BF16)
        out_ref[...] += _dot(act, wd[cols, :])
        tails.append(g[tm - SUBLANES:] if carry else g)
    tail = jnp.concatenate(tails, axis=1)
    tail_ref[0] = tail
    if carry:
        tail_s[j] = tail
    if final:
        @pl.when(j == n_steps - 1)
        def _():
            out_ref[...] = _rms(out_ref[...], fn_ref[...])


def _ffn_call(x, layer, ng, wg, wu, cw, cb, wd, prev, fn, *, tm, carry, tiles_per_seq,
              cast_own=False, u_cols=0, cast_next=None):
    n_rows = x.shape[0]
    tf, tfs = (TF_FFN, TFS_FFN) if carry else (TF_FFN_SAMPLE, TF_FFN_SAMPLE)
    n_steps = D_FF // tf
    n_tiles = n_rows // tm
    tail_rows = SUBLANES if carry else tm
    final = fn is not None
    if cast_own:
        w_specs = [pl.BlockSpec((None, D_MODEL, tf), lambda i, j: (layer, 0, j)),
                   pl.BlockSpec((None, D_MODEL, tf), lambda i, j: (layer, 0, j + n_steps)),
                   pl.BlockSpec((None, tf, D_MODEL), lambda i, j: (layer, j, 0))]
    else:
        u_off = u_cols // tf
        w_specs = [pl.BlockSpec((D_MODEL, tf), lambda i, j: (0, j)),
                   pl.BlockSpec((D_MODEL, tf), lambda i, j: (0, j + u_off)),
                   pl.BlockSpec((tf, D_MODEL), lambda i, j: (j, 0))]
    in_specs = [
        pl.BlockSpec((tm, D_MODEL), lambda i, j: (i, 0)),
        pl.BlockSpec((None, 1, D_MODEL), lambda i, j: (layer, 0, 0)),
        w_specs[0],
        w_specs[1],
        pl.BlockSpec((None, 3, tf), lambda i, j: (layer, 0, j)),
        pl.BlockSpec((None, 1, tf), lambda i, j: (layer, 0, j)),
        w_specs[2],
    ]
    args = [x, ng, wg, wu, cw, cb, wd]
    out_specs = [
        pl.BlockSpec((tm, D_MODEL), lambda i, j: (i, 0)),
        pl.BlockSpec((1, tail_rows, tf), lambda i, j: (i, 0, j)),
    ]
    out_shape = [
        jax.ShapeDtypeStruct((n_rows, D_MODEL), F32),
        jax.ShapeDtypeStruct((n_tiles, tail_rows, D_FF), F32),
    ]
    if not carry:
        in_specs.append(pl.BlockSpec((tm, tf), lambda i, j: (i, j)))
        args.append(prev)
    if final:
        in_specs.append(pl.BlockSpec((1, D_MODEL), lambda i, j: (0, 0)))
        args.append(fn)
    if cast_own:
        out_specs += [pl.BlockSpec((D_MODEL, tf), lambda i, j: (0, j)),
                      pl.BlockSpec((D_MODEL, tf), lambda i, j: (0, j)),
                      pl.BlockSpec((tf, D_MODEL), lambda i, j: (j, 0))]
        out_shape += [jax.ShapeDtypeStruct((D_MODEL, D_FF), BF16),
                      jax.ShapeDtypeStruct((D_MODEL, D_FF), BF16),
                      jax.ShapeDtypeStruct((D_FF, D_MODEL), BF16)]
    if cast_next is not None:
        total = n_tiles * n_steps
        up_w, down_h = 2 * D_FF // total, D_FF // total
        step = lambda i, j: i * n_steps + j
        in_specs += [pl.BlockSpec((None, D_MODEL, up_w), lambda i, j: (layer + 1, 0, step(i, j))),
                     pl.BlockSpec((None, down_h, D_MODEL), lambda i, j: (layer + 1, step(i, j), 0))]
        args += list(cast_next)
        out_specs += [pl.BlockSpec((D_MODEL, up_w), lambda i, j: (0, step(i, j))),
                      pl.BlockSpec((down_h, D_MODEL), lambda i, j: (step(i, j), 0))]
        out_shape += [jax.ShapeDtypeStruct((D_MODEL, 2 * D_FF), BF16),
                      jax.ShapeDtypeStruct((D_FF, D_MODEL), BF16)]
    scratch = [pltpu.VMEM((tm, D_MODEL), BF16)]
    if carry:
        scratch.append(pltpu.VMEM((n_steps, SUBLANES, tf), F32))
    return pl.pallas_call(
        functools.partial(_ffn_kernel, tm=tm, tf=tf, tfs=tfs, n_steps=n_steps,
                          tiles_per_seq=tiles_per_seq, carry=carry, final=final,
                          cast_own=cast_own, cast_next=cast_next is not None),
        grid=(n_tiles, n_steps),
        in_specs=in_specs,
        out_specs=out_specs,
        out_shape=out_shape,
        scratch_shapes=scratch,
        compiler_params=pltpu.CompilerParams(
            dimension_semantics=("arbitrary", "arbitrary"), vmem_limit_bytes=VMEM_LIMIT),
        name="conv_ffn_prompt" if carry else "conv_ffn_sample",
    )(*args)


def _rec_weight_specs(tc):
    nb = tc // RNN_BLOCK
    return [
        pl.BlockSpec((D_MODEL, tc), lambda i, j: (0, j)),
        pl.BlockSpec((D_MODEL, tc), lambda i, j: (0, j)),
        pl.BlockSpec((nb, RNN_BLOCK, RNN_BLOCK), lambda i, j: (j, 0, 0)),
        pl.BlockSpec((nb, RNN_BLOCK, RNN_BLOCK), lambda i, j: (j, 0, 0)),
        pl.BlockSpec((tc, D_MODEL), lambda i, j: (j, 0)),
    ]


def _rec_in_specs(layer, tm, tc, n_steps, stacked):
    nb = tc // RNN_BLOCK
    chunk = lambda i, j: (layer, 0, j)
    if stacked:
        w_gate, w_u, w_a, w_x, w_out = [
            pl.BlockSpec((None, D_MODEL, tc), chunk),
            pl.BlockSpec((None, D_MODEL, tc), lambda i, j: (layer, 0, j + n_steps)),
            pl.BlockSpec((None, nb, RNN_BLOCK, RNN_BLOCK), lambda i, j: (layer, j, 0, 0)),
            pl.BlockSpec((None, nb, RNN_BLOCK, RNN_BLOCK), lambda i, j: (layer, j, 0, 0)),
            pl.BlockSpec((None, tc, D_MODEL), lambda i, j: (layer, j, 0)),
        ]
    else:
        w_gate, w_u, w_a, w_x, w_out = _rec_weight_specs(tc)
    return [
        pl.BlockSpec((tm, D_MODEL), lambda i, j: (i, 0)),
        pl.BlockSpec((None, 1, D_MODEL), lambda i, j: (layer, 0, 0)),
        w_gate,
        w_u,
        pl.BlockSpec((None, 4, tc), chunk),
        pl.BlockSpec((None, 1, tc), chunk),
        w_a,
        pl.BlockSpec((None, 1, tc), chunk),
        w_x,
        pl.BlockSpec((None, 1, tc), chunk),
        pl.BlockSpec((None, 1, tc), chunk),
        w_out,
    ]


def _rec_prompt_kernel(x_ref, ng_ref, wgt_ref, wu_ref, cw_ref, cb_ref, wa_ref, ba_ref, wx_ref,
                       bx_ref, lam_ref, wo_ref, out_ref, utail_ref, hlast_ref,
                       h_s, acc_s, utail_s, hc_s, *, tm, tc, tcs, n_steps, tiles_per_seq):
    seg = tm // SUBLANES
    i = pl.program_id(0)
    j = pl.program_id(1)

    @pl.when(j == 0)
    def _():
        x = jnp.swapaxes(x_ref[...].reshape(SUBLANES, seg, D_MODEL), 0, 1)
        x = x.reshape(tm, D_MODEL)
        h_s[...] = _rms(x, ng_ref[...]).astype(BF16)
        acc_s[...] = x

    @pl.when((i % tiles_per_seq) == 0)
    def _():
        utail_s[j] = jnp.zeros((SUBLANES, tc), F32)
        hc_s[j] = jnp.zeros((SUBLANES, tc), F32)

    h = h_s[...]
    sub0 = lax.broadcasted_iota(jnp.int32, (SUBLANES, tcs), 0) == 0
    piece = lambda v, k: v[k * SUBLANES:(k + 1) * SUBLANES]
    tail_in = utail_s[j]
    hc_in = hc_s[j]
    sub_cols = [slice(sc * tcs, (sc + 1) * tcs) for sc in range(tc // tcs)]
    u_tails, h_lasts = [], []
    last = SUBLANES - 1
    us = [_dot(h, wu_ref[:, cols]) for cols in sub_cols]
    ucs_all = []
    for sc, cols in enumerate(sub_cols):
        up = [piece(us[sc], k) for k in range(seg)]
        tail = tail_in[:, cols]
        before = {}
        for d in (1, 2, 3):
            rolled = pltpu.roll(up[seg - d], 1, 0)
            before[-d] = jnp.where(sub0, tail[SUBLANES - d:SUBLANES - d + 1], rolled)
        at = lambda k: up[k] if k >= 0 else before[k]
        cw = cw_ref[:, cols]
        cb = cb_ref[:, cols]
        ucs = []
        for k in range(seg):
            v = cb + cw[0:1] * at(k - 3)
            v = v + cw[1:2] * at(k - 2)
            v = v + cw[2:3] * at(k - 1)
            ucs.append(v + cw[3:4] * up[k])
        ucs_all.append(jnp.concatenate(ucs, axis=0))
        u_tails.append(jnp.concatenate(
            [jnp.zeros((SUBLANES - 3, tcs), F32)] + [up[seg - d][last:last + 1] for d in (3, 2, 1)],
            axis=0))
    projected = []
    for sc, cols in enumerate(sub_cols):
        blocks = [sc * (tcs // RNN_BLOCK) + n for n in range(tcs // RNN_BLOCK)]
        ga, gx = _gate_dots(ucs_all[sc], wa_ref, wx_ref, blocks)
        projected.append((ga, gx, _dot(h, wgt_ref[:, cols])))
    acts = []
    for sc, cols in enumerate(sub_cols):
        ga, gx, gate = projected[sc]
        a, b = _lru_coeffs(ga, gx, ba_ref[:, cols], bx_ref[:, cols], lam_ref[:, cols],
                           ucs_all[sc])
        a_cum, b_cum = [piece(a, 0)], [piece(b, 0)]
        for k in range(1, seg):
            ak = piece(a, k)
            b_cum.append(ak * b_cum[-1] + piece(b, k))
            a_cum.append(ak * a_cum[-1])
        a_seg, b_seg = a_cum[-1], b_cum[-1]
        c = hc_in[0:1, cols]
        carries = [c]
        for s in range(SUBLANES):
            c = a_seg[s:s + 1] * c + b_seg[s:s + 1]
            carries.append(c)
        carry_in = jnp.concatenate(carries[:SUBLANES], axis=0)
        y = jnp.concatenate([a_cum[k] * carry_in + b_cum[k] for k in range(seg)], axis=0)
        acts.append((jax.nn.gelu(gate) * y).astype(BF16))
        h_lasts.append(jnp.broadcast_to(carries[SUBLANES], (SUBLANES, tcs)))
    for sc, cols in enumerate(sub_cols):
        acc_s[...] += _dot(acts[sc], wo_ref[cols, :])
    h_last = jnp.concatenate(h_lasts, axis=1)
    u_tail = jnp.concatenate(u_tails, axis=1)
    hc_s[j] = h_last
    hlast_ref[0] = h_last
    utail_s[j] = u_tail
    utail_ref[0] = u_tail

    @pl.when(j == n_steps - 1)
    def _():
        acc = jnp.swapaxes(acc_s[...].reshape(seg, SUBLANES, D_MODEL), 0, 1)
        out_ref[...] = acc.reshape(tm, D_MODEL)


def _rec_prompt_call(x, layer, prm, *, tm, tiles_per_seq):
    n_rows = x.shape[0]
    tc, tcs = TC_RNN, TCS_RNN
    n_steps = D_RNN // tc
    n_tiles = n_rows // tm
    tail_spec = pl.BlockSpec((1, SUBLANES, tc), lambda i, j: (i, 0, j))
    tail_shape = jax.ShapeDtypeStruct((n_tiles, SUBLANES, D_RNN), F32)
    return pl.pallas_call(
        functools.partial(_rec_prompt_kernel, tm=tm, tc=tc, tcs=tcs, n_steps=n_steps,
                          tiles_per_seq=tiles_per_seq),
        grid=(n_tiles, n_steps),
        in_specs=_rec_in_specs(layer, tm, tc, n_steps, stacked=False),
        out_specs=[pl.BlockSpec((tm, D_MODEL), lambda i, j: (i, 0)), tail_spec, tail_spec],
        out_shape=[jax.ShapeDtypeStruct((n_rows, D_MODEL), F32), tail_shape, tail_shape],
        scratch_shapes=[pltpu.VMEM((tm, D_MODEL), BF16), pltpu.VMEM((tm, D_MODEL), F32),
                        pltpu.VMEM((n_steps, SUBLANES, tc), F32),
                        pltpu.VMEM((n_steps, SUBLANES, tc), F32)],
        compiler_params=pltpu.CompilerParams(
            dimension_semantics=("arbitrary", "arbitrary"), vmem_limit_bytes=VMEM_LIMIT),
        name="recurrent_prompt",
    )(x, *prm)


def _rec_sample_kernel(x_ref, ng_ref, wgt_ref, wu_ref, cw_ref, cb_ref, wa_ref, ba_ref, wx_ref,
                       bx_ref, lam_ref, wo_ref, prev_ref, h0_ref, out_ref, utail_ref, ytail_ref,
                       wgt_o, wu_o, wa_o, wx_o, wo_o, h_s, *, tm, tc):
    j = pl.program_id(1)
    wgt, wu, wa, wx, wo = (w[...].astype(BF16) for w in (wgt_ref, wu_ref, wa_ref, wx_ref, wo_ref))
    for o_ref, w in zip((wgt_o, wu_o, wa_o, wx_o, wo_o), (wgt, wu, wa, wx, wo)):
        o_ref[...] = w

    @pl.when(j == 0)
    def _():
        x = x_ref[...]
        h_s[...] = _rms(x, ng_ref[...]).astype(BF16)
        out_ref[...] = x

    h = h_s[...]
    gate = _dot(h, wgt)
    u = _dot(h, wu)
    row8 = lax.broadcasted_iota(jnp.int32, (tm, tc), 0) & (GROUP_ROWS - 1)
    is_token = row8 >= STATE_ROWS
    u = jnp.where(is_token, u, prev_ref[...])
    cw = cw_ref[...]
    uc = cb_ref[...] + cw[0:1] * _shifted(u, 3, 0)
    uc = uc + cw[1:2] * _shifted(u, 2, 0)
    uc = uc + cw[2:3] * _shifted(u, 1, 0)
    uc = uc + cw[3:4] * u
    ga, gx = _gate_dots(uc, wa, wx, list(range(tc // RNN_BLOCK)))
    a, b = _lru_coeffs(ga, gx, ba_ref[...], bx_ref[...], lam_ref[...], uc)
    a = jnp.where(is_token, a, 0.0)
    b = jnp.where(is_token, b, h0_ref[...])
    for d in (1, 2, 4):
        m = row8 >= d
        a_sh = pltpu.roll(a, d, 0)
        b_sh = pltpu.roll(b, d, 0)
        b = jnp.where(m, a * b_sh + b, b)
        a = jnp.where(m, a * a_sh, a)
    act = (jax.nn.gelu(gate) * b).astype(BF16)
    out_ref[...] += _dot(act, wo)
    utail_ref[...] = u
    ytail_ref[...] = b


def _rec_sample_call(x, layer, prm, prev, h0, *, tm):
    n_rows = x.shape[0]
    tc = TC_RNN_SAMPLE
    n_steps = D_RNN // tc
    state_spec = pl.BlockSpec((tm, tc), lambda i, j: (i, j))
    state_shape = jax.ShapeDtypeStruct((n_rows, D_RNN), F32)
    return pl.pallas_call(
        functools.partial(_rec_sample_kernel, tm=tm, tc=tc),
        grid=(n_rows // tm, n_steps),
        in_specs=_rec_in_specs(layer, tm, tc, n_steps, stacked=True) + [state_spec, state_spec],
        out_specs=[pl.BlockSpec((tm, D_MODEL), lambda i, j: (i, 0)), state_spec, state_spec]
        + _rec_weight_specs(tc),
        out_shape=[jax.ShapeDtypeStruct((n_rows, D_MODEL), F32), state_shape, state_shape,
                   jax.ShapeDtypeStruct((D_MODEL, D_RNN), BF16),
                   jax.ShapeDtypeStruct((D_MODEL, D_RNN), BF16),
                   jax.ShapeDtypeStruct((D_RNN // RNN_BLOCK, RNN_BLOCK, RNN_BLOCK), BF16),
                   jax.ShapeDtypeStruct((D_RNN // RNN_BLOCK, RNN_BLOCK, RNN_BLOCK), BF16),
                   jax.ShapeDtypeStruct((D_RNN, D_MODEL), BF16)],
        scratch_shapes=[pltpu.VMEM((tm, D_MODEL), BF16)],
        compiler_params=pltpu.CompilerParams(
            dimension_semantics=("arbitrary", "arbitrary"), vmem_limit_bytes=VMEM_LIMIT),
        name="recurrent_sample",
    )(x, *prm, prev, h0)


def _kv_kernel(x_ref, ng_ref, wk_ref, wv_ref, cos_ref, slo_ref, shi_ref, k_ref, v_ref):
    h = _rms(x_ref[...], ng_ref[...]).astype(BF16)
    k = _dot(h, wk_ref[...].astype(BF16))
    k_ref[...] = _rope(k, cos_ref[...], slo_ref[...], shi_ref[...])
    v_ref[...] = _dot(h, wv_ref[...].astype(BF16))


def _kv_call(x, ng, w_k, w_v, tables, *, tm):
    n_rows = x.shape[0]
    row = lambda i: (i, 0)
    fixed = lambda i: (0, 0)
    return pl.pallas_call(
        _kv_kernel,
        grid=(n_rows // tm,),
        in_specs=[
            pl.BlockSpec((tm, D_MODEL), row),
            pl.BlockSpec((1, D_MODEL), fixed),
            pl.BlockSpec((D_MODEL, KV_LANES), fixed),
            pl.BlockSpec((D_MODEL, KV_LANES), fixed),
            pl.BlockSpec((tm, LANES), row),
            pl.BlockSpec((tm, LANES), row),
            pl.BlockSpec((tm, LANES), row),
        ],
        out_specs=[pl.BlockSpec((tm, KV_LANES), row), pl.BlockSpec((tm, KV_LANES), row)],
        out_shape=[jax.ShapeDtypeStruct((n_rows, KV_LANES), F32)] * 2,
        compiler_params=pltpu.CompilerParams(
            dimension_semantics=("arbitrary",), vmem_limit_bytes=VMEM_LIMIT),
        name="shared_kv",
    )(x, ng, w_k, w_v, *tables)


def _scores(q, kp, pair):
    rows = q.shape[0]
    hi_half = lax.broadcasted_iota(jnp.int32, (rows, LANES), 1) >= HEAD_DIM
    blocks, heads, keeps = [], [], []
    for gl in range(4):
        p = gl // 2
        keep = hi_half if p else jnp.logical_not(hi_half)
        qg = q[:, gl * LANES:(gl + 1) * LANES]
        blocks.append(jnp.where(keep, qg, 0.0))
        blocks.append(jnp.where(keep, pltpu.roll(qg, HEAD_DIM, 1), 0.0))
        g = 4 * pair + gl
        heads += [2 * g + p, 2 * g + 1 - p]
        keeps.append(keep)
    qs = jnp.concatenate(blocks, axis=0).astype(BF16)
    s = lax.dot_general(qs, kp, (((1,), (1,)), ((), ())), preferred_element_type=F32)
    return s * (HEAD_DIM ** -0.5), heads, keeps


def _attend(scored, vp, allowed, sink_ref, layer):
    s, heads, keeps = scored
    rows = s.shape[0] // 8
    probs = []
    for bi in range(8):
        sb = jnp.where(allowed, s[bi * rows:(bi + 1) * rows], -jnp.inf)
        sink = sink_ref[layer, heads[bi]]
        m = jnp.maximum(jnp.max(sb, axis=-1, keepdims=True), sink)
        pe = jnp.exp(sb - m)
        den = jnp.sum(pe, axis=-1, keepdims=True) + jnp.exp(sink - m)
        probs.append(pe / den)
    ps = jnp.concatenate(probs, axis=0).astype(BF16)
    o = _dot(ps, vp)
    outs = []
    for gl in range(4):
        o0 = o[(2 * gl) * rows:(2 * gl + 1) * rows]
        o1 = o[(2 * gl + 1) * rows:(2 * gl + 2) * rows]
        outs.append(jnp.where(keeps[gl], o0, pltpu.roll(o1, HEAD_DIM, 1)))
    return jnp.concatenate(outs, axis=1)


def _attn_kernel(*refs, layer, tm, tiles_per_seq, prompt, pps):
    it = iter(refs)
    sink_ref, x_ref, ng_ref, wq_ref, cos_ref, slo_ref, shi_ref, wo_ref = (next(it) for _ in range(8))
    ka_ref, va_ref, kb_ref, vb_ref = (next(it) for _ in range(4))
    out_ref = next(it)
    wq_o, wo_o = (None, None) if prompt else (next(it), next(it))
    h_s = next(it)
    i = pl.program_id(0)
    j = pl.program_id(1)
    if prompt:
        wq, wo = wq_ref, wo_ref[...]
    else:
        wq, wo = wq_ref[...].astype(BF16), wo_ref[...].astype(BF16)
        wq_o[...] = wq
        wo_o[...] = wo

    @pl.when(j == 0)
    def _():
        x = x_ref[...]
        h_s[...] = _rms(x, ng_ref[...]).astype(BF16)
        out_ref[...] = x

    h = h_s[...]
    rope = lambda v: _rope(v, cos_ref[...], slo_ref[...], shi_ref[...])
    blk = ATTN_BLOCK
    if prompt:
        first = (i % tiles_per_seq) == 0
        r = lax.broadcasted_iota(jnp.int32, (blk, 2 * blk), 0)
        kk = lax.broadcasted_iota(jnp.int32, (blk, 2 * blk), 1)
        band = (kk > r) & (kk <= r + blk)
        band0 = band & (kk >= jnp.where(first, blk, 0))
        qs = [_dot(h, wq[:, p * PAIR_W:(p + 1) * PAIR_W]) for p in range(pps)]
        scored, values, masks = [], [], []
        for p in range(pps):
            q = rope(qs[p])
            lanes = slice(p * LANES, (p + 1) * LANES)
            for qb in range(tm // blk):
                cur = slice(qb * blk, (qb + 1) * blk)
                if qb == 0:
                    k_prev, v_prev = kb_ref[:, lanes], vb_ref[:, lanes]
                else:
                    prv = slice((qb - 1) * blk, qb * blk)
                    k_prev, v_prev = ka_ref[prv, lanes], va_ref[prv, lanes]
                masks.append(band0 if qb == 0 else band)
                kp = jnp.concatenate([k_prev, ka_ref[cur, lanes]], axis=0).astype(BF16)
                values.append(
                    jnp.concatenate([v_prev, va_ref[cur, lanes]], axis=0).astype(BF16))
                scored.append(_scores(q[cur], kp, j * pps + p))
        outs = [_attend(sc, vp, allowed, sink_ref, layer)
                for sc, vp, allowed in zip(scored, values, masks)]
        per = tm // blk
        o = jnp.concatenate(
            [jnp.concatenate(outs[p * per:(p + 1) * per], axis=0) for p in range(pps)], axis=1)
    else:
        q = rope(_dot(h, wq))
        n_groups = tm // GROUP_ROWS
        t = lax.broadcasted_iota(jnp.int32, (GROUP_ROWS, 2 * blk), 0) - STATE_ROWS
        kk = lax.broadcasted_iota(jnp.int32, (GROUP_ROWS, 2 * blk), 1)
        new_t = kk - blk - STATE_ROWS
        allowed = ((kk < blk) & (kk > t)) | ((kk >= blk) & (new_t >= 0) & (new_t <= t) &
                                               (kk < blk + GROUP_ROWS))
        allowed = allowed & (t >= 0)
        pad = jnp.zeros((blk - GROUP_ROWS, LANES), F32)
        scored, values = [], []
        for b in range(n_groups):
            rows = slice(b * GROUP_ROWS, (b + 1) * GROUP_ROWS)
            kp = jnp.concatenate([ka_ref[b], kb_ref[rows, :], pad], axis=0).astype(BF16)
            values.append(jnp.concatenate([va_ref[b], vb_ref[rows, :], pad], axis=0).astype(BF16))
            scored.append(_scores(q[rows], kp, j))
        o = jnp.concatenate(
            [_attend(sc, vp, allowed, sink_ref, layer) for sc, vp in zip(scored, values)], axis=0)
    out_ref[...] += _dot(o.astype(BF16), wo)


def _attn_call(x, layer, ng, w_q, w_o, sinks, tables, ka, va, kb, vb, *, tm, tiles_per_seq, prompt):
    n_rows = x.shape[0]
    n_tiles = n_rows // tm
    pps = PAIRS_PER_STEP if prompt else 1
    row = lambda i, j: (i, 0)
    if prompt:
        per = tm // ATTN_BLOCK
        cur_spec = pl.BlockSpec((tm, pps * LANES), lambda i, j: (i, j))
        prev_spec = pl.BlockSpec((ATTN_BLOCK, pps * LANES),
                                 lambda i, j: (jnp.maximum(i * per - 1, 0), j))
        kv_specs = [cur_spec, cur_spec, prev_spec, prev_spec]
    else:
        n_groups = tm // GROUP_ROWS
        cache_spec = pl.BlockSpec((n_groups, ATTN_BLOCK, LANES), lambda i, j: (0, 0, j))
        new_spec = pl.BlockSpec((tm, LANES), lambda i, j: (i, j))
        kv_specs = [cache_spec, cache_spec, new_spec, new_spec]
    wq_copy = pl.BlockSpec((D_MODEL, pps * PAIR_W), lambda i, j: (0, j))
    wo_copy = pl.BlockSpec((pps * PAIR_W, D_MODEL), lambda i, j: (j, 0))
    x_shape = jax.ShapeDtypeStruct((n_rows, D_MODEL), F32)
    if prompt:
        wq_spec, wo_spec = wq_copy, wo_copy
        out_specs, out_shape = pl.BlockSpec((tm, D_MODEL), row), x_shape
    else:
        wq_spec = pl.BlockSpec((None, D_MODEL, PAIR_W), lambda i, j: (layer, 0, j))
        wo_spec = pl.BlockSpec((None, PAIR_W, D_MODEL), lambda i, j: (layer, j, 0))
        out_specs = [pl.BlockSpec((tm, D_MODEL), row), wq_copy, wo_copy]
        out_shape = [x_shape, jax.ShapeDtypeStruct((D_MODEL, N_HEADS * HEAD_DIM), BF16),
                     jax.ShapeDtypeStruct((N_HEADS * HEAD_DIM, D_MODEL), BF16)]
    return pl.pallas_call(
        functools.partial(_attn_kernel, layer=layer, tm=tm, tiles_per_seq=tiles_per_seq,
                          prompt=prompt, pps=pps),
        grid=(n_tiles, N_PAIRS // pps),
        in_specs=[
            pl.BlockSpec(memory_space=pltpu.SMEM),
            pl.BlockSpec((tm, D_MODEL), row),
            pl.BlockSpec((None, 1, D_MODEL), lambda i, j: (layer, 0, 0)),
            wq_spec,
            pl.BlockSpec((tm, LANES), row),
            pl.BlockSpec((tm, LANES), row),
            pl.BlockSpec((tm, LANES), row),
            wo_spec,
        ] + kv_specs,
        out_specs=out_specs,
        out_shape=out_shape,
        scratch_shapes=[pltpu.VMEM((tm, D_MODEL), BF16)],
        compiler_params=pltpu.CompilerParams(
            dimension_semantics=("arbitrary", "arbitrary"), vmem_limit_bytes=VMEM_LIMIT),
        name="attention_prompt" if prompt else "attention_sample",
    )(sinks, x, ng, w_q, *tables, w_o, ka, va, kb, vb)


def _rope_tables(pos):
    half = HEAD_DIM // 2
    freqs = ROPE_THETA ** (-jnp.arange(half, dtype=F32) / half)
    ang = pos.astype(F32)[:, None] * freqs[None, :]
    cos, sin = jnp.cos(ang), jnp.sin(ang)
    zero = jnp.zeros_like(sin)
    cos_t = jnp.concatenate([cos, cos] * 2, axis=1)
    sin_lo = jnp.concatenate([-sin, zero] * 2, axis=1)
    sin_hi = jnp.concatenate([zero, sin] * 2, axis=1)
    return cos_t, sin_lo, sin_hi


def _to_groups(state, n_state):
    n, b, _, c = state.shape
    padded = jnp.pad(
        state, ((0, 0), (0, 0), (STATE_ROWS - n_state, GROUP_ROWS - STATE_ROWS), (0, 0)))
    return padded.reshape(n, b * GROUP_ROWS, c)


def kernel(x_prompt, x_sample, state_rglru_h, state_rglru_conv, state_ffn_conv, cache_window_k,
           cache_window_v, a_norm, a_w_in, a_conv_w, a_conv_b, a_gate_a_w, a_gate_a_b, a_gate_x_w,
           a_gate_x_b, a_lambda, a_w_out, kv_norm, w_k, w_v, b_norm, w_q, sinks, w_o, f_norm,
           f_w_up, f_conv_w, f_conv_b, f_w_down, final_norm):
    bp, s_len, _ = x_prompt.shape
    bs, t_len, _ = x_sample.shape
    depth = f_norm.shape[0]
    n_a = a_norm.shape[0]
    assert t_len == GROUP_ROWS - STATE_ROWS and cache_window_k.shape[1] == ATTN_BLOCK
    tiles_per_seq = s_len // TM_PROMPT
    ts = bs * GROUP_ROWS

    xp = x_prompt.reshape(bp * s_len, D_MODEL)
    xs = jnp.pad(x_sample, ((0, 0), (STATE_ROWS, 0), (0, 0))).reshape(ts, D_MODEL)
    pos_p = jnp.tile(jnp.arange(s_len, dtype=jnp.int32), bp)
    pos_s = jnp.tile(PAST_LEN - STATE_ROWS + jnp.arange(GROUP_ROWS, dtype=jnp.int32), bs)
    tab_p = _rope_tables(pos_p)
    tab_s = _rope_tables(pos_s)

    vec = lambda v: v.reshape(v.shape[:-1] + (1, v.shape[-1]))
    a_norm3, a_conv_b3, a_lambda3 = vec(a_norm), vec(a_conv_b), vec(a_lambda)
    a_gate_a_b3, a_gate_x_b3 = vec(a_gate_a_b), vec(a_gate_x_b)
    f_norm3, f_conv_b3, b_norm3 = vec(f_norm), vec(f_conv_b), vec(b_norm)

    def rec_params(w_gate_in, w_u_in, w_a, w_x, w_out):
        return (a_norm3, w_gate_in, w_u_in, a_conv_w, a_conv_b3, w_a, a_gate_a_b3, w_x,
                a_gate_x_b3, a_lambda3, w_out)
    cache_k = cache_window_k.reshape(bs, ATTN_BLOCK, KV_LANES)
    cache_v = cache_window_v.reshape(bs, ATTN_BLOCK, KV_LANES)
    rec_prev = _to_groups(state_rglru_conv, 3)
    rec_h0 = _to_groups(state_rglru_h[:, :, None, :], 1)
    ffn_prev = _to_groups(state_ffn_conv, 2)
    last = slice(tiles_per_seq - 1, None, tiles_per_seq)

    h_p, h_s, c_p, c_s, f_p, f_s = [], [], [], [], [], []
    for layer in range(depth):
        if layer < n_a:
            prm = rec_params(a_w_in, a_w_in, a_gate_a_w, a_gate_x_w, a_w_out)
            xs, ut, yt, *w_b = _rec_sample_call(xs, layer, prm, rec_prev[layer], rec_h0[layer],
                                                tm=ts)
            c_s.append(ut.reshape(bs, GROUP_ROWS, D_RNN)[:, GROUP_ROWS - 3:])
            h_s.append(yt.reshape(bs, GROUP_ROWS, D_RNN)[:, GROUP_ROWS - 1])
            xp, ut, ht = _rec_prompt_call(xp, layer, rec_params(*w_b), tm=TM_PROMPT,
                                          tiles_per_seq=tiles_per_seq)
            c_p.append(ut[last, SUBLANES - 3:])
            h_p.append(ht[last, 0])
        else:
            j = layer - n_a
            if j == 0:
                k_s, v_s = _kv_call(xs, vec(kv_norm), w_k, w_v, tab_s, tm=ts)
                k_p, v_p = _kv_call(xp, vec(kv_norm), w_k, w_v, tab_p, tm=TM_PROMPT)
            xs, w_q_b, w_o_b = _attn_call(xs, j, b_norm3, w_q, w_o, sinks, tab_s, cache_k, cache_v,
                                          k_s, v_s, tm=ts, tiles_per_seq=1, prompt=False)
            xp = _attn_call(xp, j, b_norm3, w_q_b, w_o_b, sinks, tab_p, k_p, v_p, k_p, v_p,
                            tm=TM_PROMPT, tiles_per_seq=tiles_per_seq, prompt=True)
        fn = vec(final_norm) if layer == depth - 1 else None
        if layer == 0:
            xs, gt, wg_b, wu_b, wd_b = _ffn_call(
                xs, layer, f_norm3, f_w_up, f_w_up, f_conv_w, f_conv_b3, f_w_down, ffn_prev[layer],
                fn, tm=ts, carry=False, tiles_per_seq=1, cast_own=True)
            u_cols = 0
        else:
            xs, gt = _ffn_call(xs, layer, f_norm3, wg_b, wu_b, f_conv_w, f_conv_b3, wd_b,
                               ffn_prev[layer], fn, tm=ts, carry=False, tiles_per_seq=1,
                               u_cols=u_cols)
        f_s.append(gt.reshape(bs, GROUP_ROWS, D_FF)[:, GROUP_ROWS - 2:])
        cast_next = (f_w_up, f_w_down) if layer + 1 < depth else None
        xp, gt, *w_next = _ffn_call(xp, layer, f_norm3, wg_b, wu_b, f_conv_w, f_conv_b3, wd_b, None,
                                    fn, tm=TM_PROMPT, carry=True, tiles_per_seq=tiles_per_seq,
                                    u_cols=u_cols, cast_next=cast_next)
        f_p.append(gt[last, SUBLANES - 2:])
        if w_next:
            wg_b, wu_b, wd_b, u_cols = w_next[0], w_next[0], w_next[1], D_FF

    y_prompt = xp.reshape(bp, s_len, D_MODEL)
    y_sample = xs.reshape(bs, GROUP_ROWS, D_MODEL)[:, STATE_ROWS:]
    heads = lambda a: a.reshape(a.shape[0], a.shape[1], N_KV_HEADS, HEAD_DIM)
    win_p = min(ATTN_BLOCK, s_len)
    k_p3 = k_p.reshape(bp, s_len, KV_LANES)
    v_p3 = v_p.reshape(bp, s_len, KV_LANES)
    k_new = k_s.reshape(bs, GROUP_ROWS, KV_LANES)[:, STATE_ROWS:]
    v_new = v_s.reshape(bs, GROUP_ROWS, KV_LANES)[:, STATE_ROWS:]
    k_all = jnp.concatenate([cache_k, k_new], axis=1)
    v_all = jnp.concatenate([cache_v, v_new], axis=1)
    return (y_prompt, y_sample, jnp.stack(h_p), jnp.stack(h_s), jnp.stack(c_p), jnp.stack(c_s),
            jnp.stack(f_p), jnp.stack(f_s), heads(k_p3[:, s_len - win_p:]),
            heads(v_p3[:, s_len - win_p:]), heads(k_all[:, t_len:]), heads(v_all[:, t_len:]))
```

```python
import functools

import jax
import jax.numpy as jnp
from jax import lax
from jax.experimental import pallas as pl
from jax.experimental.pallas import tpu as pltpu

F32 = jnp.float32
BF16 = jnp.bfloat16

D_MODEL = 2048
PAST_LEN = 16384
D_RNN = 2048
RNN_BLOCK = 256
LRU_C = 8.0
N_HEADS = 32
HEAD_DIM = 64
N_KV_HEADS = 8
ATTN_BLOCK = 128
ROPE_THETA = 10000.0
D_FF = 3 * D_MODEL
EPS = 1e-6

LANES = 128
SUBLANES = 8
GROUP_ROWS = SUBLANES
STATE_ROWS = 4
KV_LANES = N_KV_HEADS * HEAD_DIM
PAIR_W = 4 * LANES
N_PAIRS = N_KV_HEADS // 2
PAIRS_PER_STEP = 2
VMEM_LIMIT = 56 * 1024 * 1024

TM_PROMPT = 512
SEG = TM_PROMPT // SUBLANES
TC_RNN = 512
TCS_RNN = 256
TC_RNN_SAMPLE = 256
TF_FFN = 1024
TFS_FFN = 512
TF_FFN_SAMPLE = 512


def _rms(x, g):
    ms = jnp.mean(x * x, axis=-1, keepdims=True)
    return x * lax.rsqrt(ms + EPS) * g


def _dot(a, b):
    return jnp.dot(a, b, preferred_element_type=F32)


def _shifted(xe, k, drop):
    return pltpu.roll(xe, k, 0)[drop:]


def _rope(x, cos, sin_lo, sin_hi):
    w = x.shape[1]
    n = w // LANES
    cosw = jnp.concatenate([cos] * n, axis=1)
    lo = jnp.concatenate([sin_lo] * n, axis=1)
    hi = jnp.concatenate([sin_hi] * n, axis=1)
    half = HEAD_DIM // 2
    return x * cosw + pltpu.roll(x, w - half, 1) * lo + pltpu.roll(x, half, 1) * hi


def _state_rows(state):
    g, n, c = state.shape
    padded = jnp.pad(state, ((0, 0), (STATE_ROWS - n, GROUP_ROWS - STATE_ROWS), (0, 0)))
    return padded.reshape(g * GROUP_ROWS, c)


def _after(w, dep):
    bits = pltpu.bitcast(dep[0:SUBLANES, 0:LANES], jnp.uint32)
    zero = lax.shift_right_logical(lax.shift_right_logical(bits, jnp.uint32(16)), jnp.uint32(16))
    rows = 2 * SUBLANES
    parts = []
    for r0 in range(0, w.shape[0], 256):
        head = pltpu.bitcast(pltpu.bitcast(w[r0:r0 + rows, 0:LANES], jnp.uint32) | zero, BF16)
        if w.shape[1] > LANES:
            head = jnp.concatenate([head, w[r0:r0 + rows, LANES:]], axis=1)
        parts += [head, w[r0 + rows:min(r0 + 256, w.shape[0])]]
    return jnp.concatenate(parts, axis=0)


def _gate_dots(uc, wa_ref, wx_ref, blocks):
    ucb = uc.astype(BF16)
    ga = jnp.concatenate(
        [_dot(ucb[:, n * RNN_BLOCK:(n + 1) * RNN_BLOCK], wa_ref[bk]) for n, bk in enumerate(blocks)],
        axis=1)
    gx = jnp.concatenate(
        [_dot(ucb[:, n * RNN_BLOCK:(n + 1) * RNN_BLOCK], wx_ref[bk]) for n, bk in enumerate(blocks)],
        axis=1)
    return ga, gx


def _lru_coeffs(ga, gx, ba, bx, lam, uc):
    r = jax.nn.sigmoid(ga + ba)
    ig = jax.nn.sigmoid(gx + bx)
    z = -lam
    softplus = jnp.maximum(z, 0.0) + jnp.log1p(jnp.exp(-jnp.abs(z)))
    log_a = (-LRU_C * r) * softplus
    a = jnp.exp(log_a)
    sq = -jnp.tanh(log_a) * (a * a + 1.0)
    mult = jnp.where(sq > 0.0, sq * lax.rsqrt(sq), 0.0)
    b = mult * ig * uc
    return a, b


def _ffn_kernel(*refs, tm, tf, tfs, n_steps, tiles_per_seq, carry, final, cast_own, cast_next):
    it = iter(refs)
    x_ref, ng_ref, wg_ref, wu_ref, cw_ref, cb_ref, wd_ref = (next(it) for _ in range(7))
    prev_ref = None if carry else next(it)
    fn_ref = next(it) if final else None
    next_src = (next(it), next(it)) if cast_next else ()
    out_ref, tail_ref = next(it), next(it)
    w_out_refs = (next(it), next(it), next(it)) if cast_own else None
    next_dst = (next(it), next(it)) if cast_next else ()
    h_s = next(it)
    tail_s = next(it) if carry else None
    i = pl.program_id(0)
    j = pl.program_id(1)

    @pl.when(j == 0)
    def _():
        x = x_ref[...]
        h_s[...] = _rms(x, ng_ref[...]).astype(BF16)
        out_ref[...] = x

    if carry:
        @pl.when((i % tiles_per_seq) == 0)
        def _():
            tail_s[j] = jnp.zeros((SUBLANES, tf), F32)
    else:
        row = lax.broadcasted_iota(jnp.int32, (tm, tfs), 0)
        is_token = (row & (GROUP_ROWS - 1)) >= STATE_ROWS
    wg, wu, wd = wg_ref, wu_ref, wd_ref
    if cast_own:
        wg, wu, wd = (w[...].astype(BF16) for w in (wg_ref, wu_ref, wd_ref))
        for o_ref, w in zip(w_out_refs, (wg, wu, wd)):
            o_ref[...] = w
    for src, dst in zip(next_src, next_dst):
        dst[...] = src[...].astype(BF16)

    h = h_s[...]
    tail_in = tail_s[j] if carry else None
    tails = []
    for sc in range(tf // tfs):
        cols = slice(sc * tfs, (sc + 1) * tfs)
        g = _dot(h, wg[:, cols])
        u = _dot(h, wu[:, cols])
        if carry:
            ge = jnp.concatenate([tail_in[:, cols], g], axis=0)
            drop = SUBLANES
        else:
            g = jnp.where(is_token, g, _state_rows(prev_ref[:, :, cols]))
            ge = g
            drop = 0
        gc = cb_ref[:, cols] + cw_ref[0:1, cols] * _shifted(ge, 2, drop)
        gc = gc + cw_ref[1:2, cols] * _shifted(ge, 1, drop)
        gc = gc + cw_ref[2:3, cols] * g
        act = (jax.nn.gelu(gc) * u).astype(BF16)
        out_ref[...] += _dot(act, wd[cols, :])
        tails.append(g[tm - SUBLANES:] if carry else g)
    tail = jnp.concatenate(tails, axis=1)
    tail_ref[0] = tail
    if carry:
        tail_s[j] = tail
    if final:
        @pl.when(j == n_steps - 1)
        def _():
            out_ref[...] = _rms(out_ref[...], fn_ref[...])


def _ffn_call(x, layer, ng, wg, wu, cw, cb, wd, prev, fn, *, tm, carry, tiles_per_seq,
              cast_own=False, u_cols=0, cast_next=None):
    n_rows = x.shape[0]
    tf, tfs = (TF_FFN, TFS_FFN) if carry else (TF_FFN_SAMPLE, TF_FFN_SAMPLE)
    n_steps = D_FF // tf
    n_tiles = n_rows // tm
    tail_rows = SUBLANES if carry else tm
    final = fn is not None
    if cast_own:
        w_specs = [pl.BlockSpec((None, D_MODEL, tf), lambda i, j: (layer, 0, j)),
                   pl.BlockSpec((None, D_MODEL, tf), lambda i, j: (layer, 0, j + n_steps)),
                   pl.BlockSpec((None, tf, D_MODEL), lambda i, j: (layer, j, 0))]
    else:
        u_off = u_cols // tf
        w_specs = [pl.BlockSpec((D_MODEL, tf), lambda i, j: (0, j)),
                   pl.BlockSpec((D_MODEL, tf), lambda i, j: (0, j + u_off)),
                   pl.BlockSpec((tf, D_MODEL), lambda i, j: (j, 0))]
    in_specs = [
        pl.BlockSpec((tm, D_MODEL), lambda i, j: (i, 0)),
        pl.BlockSpec((None, 1, D_MODEL), lambda i, j: (layer, 0, 0)),
        w_specs[0],
        w_specs[1],
        pl.BlockSpec((None, 3, tf), lambda i, j: (layer, 0, j)),
        pl.BlockSpec((None, 1, tf), lambda i, j: (layer, 0, j)),
        w_specs[2],
    ]
    args = [x, ng, wg, wu, cw, cb, wd]
    out_specs = [
        pl.BlockSpec((tm, D_MODEL), lambda i, j: (i, 0)),
        pl.BlockSpec((1, tail_rows, tf), lambda i, j: (i, 0, j)),
    ]
    out_shape = [
        jax.ShapeDtypeStruct((n_rows, D_MODEL), F32),
        jax.ShapeDtypeStruct((n_tiles, tail_rows, D_FF), F32),
    ]
    if not carry:
        n_groups, n_state = prev.shape[1:3]
        in_specs.append(pl.BlockSpec((None, n_groups, n_state, tf), lambda i, j: (layer, 0, 0, j)))
        args.append(prev)
    if final:
        in_specs.append(pl.BlockSpec((1, D_MODEL), lambda i, j: (0, 0)))
        args.append(fn)
    if cast_own:
        out_specs += [pl.BlockSpec((D_MODEL, tf), lambda i, j: (0, j)),
                      pl.BlockSpec((D_MODEL, tf), lambda i, j: (0, j)),
                      pl.BlockSpec((tf, D_MODEL), lambda i, j: (j, 0))]
        out_shape += [jax.ShapeDtypeStruct((D_MODEL, D_FF), BF16),
                      jax.ShapeDtypeStruct((D_MODEL, D_FF), BF16),
                      jax.ShapeDtypeStruct((D_FF, D_MODEL), BF16)]
    if cast_next is not None:
        total = n_tiles * n_steps
        up_w, down_h = 2 * D_FF // total, D_FF // total
        step = lambda i, j: i * n_steps + j
        in_specs += [pl.BlockSpec((None, D_MODEL, up_w), lambda i, j: (layer + 1, 0, step(i, j))),
                     pl.BlockSpec((None, down_h, D_MODEL), lambda i, j: (layer + 1, step(i, j), 0))]
        args += list(cast_next)
        out_specs += [pl.BlockSpec((D_MODEL, up_w), lambda i, j: (0, step(i, j))),
                      pl.BlockSpec((down_h, D_MODEL), lambda i, j: (step(i, j), 0))]
        out_shape += [jax.ShapeDtypeStruct((D_MODEL, 2 * D_FF), BF16),
                      jax.ShapeDtypeStruct((D_FF, D_MODEL), BF16)]
    scratch = [pltpu.VMEM((tm, D_MODEL), BF16)]
    if carry:
        scratch.append(pltpu.VMEM((n_steps, SUBLANES, tf), F32))
    return pl.pallas_call(
        functools.partial(_ffn_kernel, tm=tm, tf=tf, tfs=tfs, n_steps=n_steps,
                          tiles_per_seq=tiles_per_seq, carry=carry, final=final,
                          cast_own=cast_own, cast_next=cast_next is not None),
        grid=(n_tiles, n_steps),
        in_specs=in_specs,
        out_specs=out_specs,
        out_shape=out_shape,
        scratch_shapes=scratch,
        compiler_params=pltpu.CompilerParams(
            dimension_semantics=("arbitrary", "arbitrary"), vmem_limit_bytes=VMEM_LIMIT),
        name="conv_ffn_prompt" if carry else "conv_ffn_sample",
    )(*args)


def _rec_weight_specs(tc):
    nb = tc // RNN_BLOCK
    return [
        pl.BlockSpec((D_MODEL, tc), lambda i, j: (0, j)),
        pl.BlockSpec((D_MODEL, tc), lambda i, j: (0, j)),
        pl.BlockSpec((nb, RNN_BLOCK, RNN_BLOCK), lambda i, j: (j, 0, 0)),
        pl.BlockSpec((nb, RNN_BLOCK, RNN_BLOCK), lambda i, j: (j, 0, 0)),
        pl.BlockSpec((tc, D_MODEL), lambda i, j: (j, 0)),
    ]


def _rec_in_specs(layer, tm, tc, n_steps, stacked):
    nb = tc // RNN_BLOCK
    chunk = lambda i, j: (layer, 0, j)
    if stacked:
        w_gate, w_u, w_a, w_x, w_out = [
            pl.BlockSpec((None, D_MODEL, tc), chunk),
            pl.BlockSpec((None, D_MODEL, tc), lambda i, j: (layer, 0, j + n_steps)),
            pl.BlockSpec((None, nb, RNN_BLOCK, RNN_BLOCK), lambda i, j: (layer, j, 0, 0)),
            pl.BlockSpec((None, nb, RNN_BLOCK, RNN_BLOCK), lambda i, j: (layer, j, 0, 0)),
            pl.BlockSpec((None, tc, D_MODEL), lambda i, j: (layer, j, 0)),
        ]
    else:
        w_gate, w_u, w_a, w_x, w_out = _rec_weight_specs(tc)
    return [
        pl.BlockSpec((tm, D_MODEL), lambda i, j: (i, 0)),
        pl.BlockSpec((None, 1, D_MODEL), lambda i, j: (layer, 0, 0)),
        w_gate,
        w_u,
        pl.BlockSpec((None, 4, tc), chunk),
        pl.BlockSpec((None, 1, tc), chunk),
        w_a,
        pl.BlockSpec((None, 1, tc), chunk),
        w_x,
        pl.BlockSpec((None, 1, tc), chunk),
        pl.BlockSpec((None, 1, tc), chunk),
        w_out,
    ]


def _rec_prompt_kernel(x_ref, ng_ref, wgt_ref, wu_ref, cw_ref, cb_ref, wa_ref, ba_ref, wx_ref,
                       bx_ref, lam_ref, wo_ref, out_ref, utail_ref, hlast_ref,
                       h_s, utail_s, hc_s, *, tm, tc, tcs, n_steps, tiles_per_seq):
    seg = tm // SUBLANES
    i = pl.program_id(0)
    j = pl.program_id(1)

    @pl.when(j == 0)
    def _():
        x = jnp.swapaxes(x_ref[...].reshape(SUBLANES, seg, D_MODEL), 0, 1)
        x = x.reshape(tm, D_MODEL)
        h_s[...] = _rms(x, ng_ref[...]).astype(BF16)
        out_ref[...] = x

    @pl.when((i % tiles_per_seq) == 0)
    def _():
        utail_s[j] = jnp.zeros((SUBLANES, tc), F32)
        hc_s[j] = jnp.zeros((SUBLANES, tc), F32)

    h = h_s[...]
    sub0 = lax.broadcasted_iota(jnp.int32, (SUBLANES, tcs), 0) == 0
    piece = lambda v, k: v[k * SUBLANES:(k + 1) * SUBLANES]
    tail_in = utail_s[j]
    hc_in = hc_s[j]
    sub_cols = [slice(sc * tcs, (sc + 1) * tcs) for sc in range(tc // tcs)]
    u_tails, h_lasts = [], []
    last = SUBLANES - 1
    us = [_dot(h, wu_ref[:, cols]) for cols in sub_cols]
    ucs_all = []
    for sc, cols in enumerate(sub_cols):
        up = [piece(us[sc], k) for k in range(seg)]
        tail = tail_in[:, cols]
        before = {}
        for d in (1, 2, 3):
            rolled = pltpu.roll(up[seg - d], 1, 0)
            before[-d] = jnp.where(sub0, tail[SUBLANES - d:SUBLANES - d + 1], rolled)
        at = lambda k: up[k] if k >= 0 else before[k]
        cw = cw_ref[:, cols]
        cb = cb_ref[:, cols]
        ucs = []
        for k in range(seg):
            v = cb + cw[0:1] * at(k - 3)
            v = v + cw[1:2] * at(k - 2)
            v = v + cw[2:3] * at(k - 1)
            ucs.append(v + cw[3:4] * up[k])
        ucs_all.append(jnp.concatenate(ucs, axis=0))
        u_tails.append(jnp.concatenate(
            [jnp.zeros((SUBLANES - 3, tcs), F32)] + [up[seg - d][last:last + 1] for d in (3, 2, 1)],
            axis=0))
    projected = []
    for sc, cols in enumerate(sub_cols):
        blocks = [sc * (tcs // RNN_BLOCK) + n for n in range(tcs // RNN_BLOCK)]
        ga, gx = _gate_dots(ucs_all[sc], wa_ref, wx_ref, blocks)
        projected.append((ga, gx, _dot(h, _after(wgt_ref[:, cols], ga))))
    acts = []
    for sc, cols in enumerate(sub_cols):
        ga, gx, gate = projected[sc]
        a, b = _lru_coeffs(ga, gx, ba_ref[:, cols], bx_ref[:, cols], lam_ref[:, cols],
                           ucs_all[sc])
        a_cum, b_cum = [piece(a, 0)], [piece(b, 0)]
        for k in range(1, seg):
            ak = piece(a, k)
            b_cum.append(ak * b_cum[-1] + piece(b, k))
            a_cum.append(ak * a_cum[-1])
        a_seg, b_seg = a_cum[-1], b_cum[-1]
        c = hc_in[0:1, cols]
        carries = [c]
        for s in range(SUBLANES):
            c = a_seg[s:s + 1] * c + b_seg[s:s + 1]
            carries.append(c)
        carry_in = jnp.concatenate(carries[:SUBLANES], axis=0)
        y = jnp.concatenate([a_cum[k] * carry_in + b_cum[k] for k in range(seg)], axis=0)
        acts.append((jax.nn.gelu(gate) * y).astype(BF16))
        h_lasts.append(jnp.broadcast_to(carries[SUBLANES], (SUBLANES, tcs)))
    for sc, cols in enumerate(sub_cols):
        out_ref[...] += _dot(acts[sc], wo_ref[cols, :])
    h_last = jnp.concatenate(h_lasts, axis=1)
    u_tail = jnp.concatenate(u_tails, axis=1)
    hc_s[j] = h_last
    hlast_ref[0] = h_last
    utail_s[j] = u_tail
    utail_ref[0] = u_tail

    @pl.when(j == n_steps - 1)
    def _():
        acc = jnp.swapaxes(out_ref[...].reshape(seg, SUBLANES, D_MODEL), 0, 1)
        out_ref[...] = acc.reshape(tm, D_MODEL)


def _rec_prompt_call(x, layer, prm, *, tm, tiles_per_seq):
    n_rows = x.shape[0]
    tc, tcs = TC_RNN, TCS_RNN
    n_steps = D_RNN // tc
    n_tiles = n_rows // tm
    tail_spec = pl.BlockSpec((1, SUBLANES, tc), lambda i, j: (i, 0, j))
    tail_shape = jax.ShapeDtypeStruct((n_tiles, SUBLANES, D_RNN), F32)
    return pl.pallas_call(
        functools.partial(_rec_prompt_kernel, tm=tm, tc=tc, tcs=tcs, n_steps=n_steps,
                          tiles_per_seq=tiles_per_seq),
        grid=(n_tiles, n_steps),
        in_specs=_rec_in_specs(layer, tm, tc, n_steps, stacked=False),
        out_specs=[pl.BlockSpec((tm, D_MODEL), lambda i, j: (i, 0)), tail_spec, tail_spec],
        out_shape=[jax.ShapeDtypeStruct((n_rows, D_MODEL), F32), tail_shape, tail_shape],
        scratch_shapes=[pltpu.VMEM((tm, D_MODEL), BF16),
                        pltpu.VMEM((n_steps, SUBLANES, tc), F32),
                        pltpu.VMEM((n_steps, SUBLANES, tc), F32)],
        compiler_params=pltpu.CompilerParams(
            dimension_semantics=("arbitrary", "arbitrary"), vmem_limit_bytes=VMEM_LIMIT),
        name="recurrent_prompt",
    )(x, *prm)


def _rec_sample_kernel(x_ref, ng_ref, wgt_ref, wu_ref, cw_ref, cb_ref, wa_ref, ba_ref, wx_ref,
                       bx_ref, lam_ref, wo_ref, prev_ref, h0_ref, out_ref, utail_ref, ytail_ref,
                       wgt_o, wu_o, wa_o, wx_o, wo_o, h_s, *, tm, tc):
    j = pl.program_id(1)
    wgt, wu, wa, wx, wo = (w[...].astype(BF16) for w in (wgt_ref, wu_ref, wa_ref, wx_ref, wo_ref))
    for o_ref, w in zip((wgt_o, wu_o, wa_o, wx_o, wo_o), (wgt, wu, wa, wx, wo)):
        o_ref[...] = w

    @pl.when(j == 0)
    def _():
        x = x_ref[...]
        h_s[...] = _rms(x, ng_ref[...]).astype(BF16)
        out_ref[...] = x

    h = h_s[...]
    gate = _dot(h, wgt)
    u = _dot(h, wu)
    row8 = lax.broadcasted_iota(jnp.int32, (tm, tc), 0) & (GROUP_ROWS - 1)
    is_token = row8 >= STATE_ROWS
    u = jnp.where(is_token, u, _state_rows(prev_ref[...]))
    cw = cw_ref[...]
    uc = cb_ref[...] + cw[0:1] * _shifted(u, 3, 0)
    uc = uc + cw[1:2] * _shifted(u, 2, 0)
    uc = uc + cw[2:3] * _shifted(u, 1, 0)
    uc = uc + cw[3:4] * u
    ga, gx = _gate_dots(uc, wa, wx, list(range(tc // RNN_BLOCK)))
    a, b = _lru_coeffs(ga, gx, ba_ref[...], bx_ref[...], lam_ref[...], uc)
    a = jnp.where(is_token, a, 0.0)
    b = jnp.where(is_token, b, _state_rows(h0_ref[...]))
    for d in (1, 2, 4):
        m = row8 >= d
        a_sh = pltpu.roll(a, d, 0)
        b_sh = pltpu.roll(b, d, 0)
        b = jnp.where(m, a * b_sh + b, b)
        a = jnp.where(m, a * a_sh, a)
    act = (jax.nn.gelu(gate) * b).astype(BF16)
    out_ref[...] += _dot(act, wo)
    utail_ref[...] = u
    ytail_ref[...] = b


def _rec_sample_call(x, layer, prm, prev, h0, *, tm):
    n_rows = x.shape[0]
    tc = TC_RNN_SAMPLE
    n_steps = D_RNN // tc
    state_spec = pl.BlockSpec((tm, tc), lambda i, j: (i, j))
    state_shape = jax.ShapeDtypeStruct((n_rows, D_RNN), F32)
    return pl.pallas_call(
        functools.partial(_rec_sample_kernel, tm=tm, tc=tc),
        grid=(n_rows // tm, n_steps),
        in_specs=_rec_in_specs(layer, tm, tc, n_steps, stacked=True) + [
            pl.BlockSpec((None,) + prev.shape[1:3] + (tc,), lambda i, j: (layer, 0, 0, j)),
            pl.BlockSpec((None,) + h0.shape[1:3] + (tc,), lambda i, j: (layer, 0, 0, j))],
        out_specs=[pl.BlockSpec((tm, D_MODEL), lambda i, j: (i, 0)), state_spec, state_spec]
        + _rec_weight_specs(tc),
        out_shape=[jax.ShapeDtypeStruct((n_rows, D_MODEL), F32), state_shape, state_shape,
                   jax.ShapeDtypeStruct((D_MODEL, D_RNN), BF16),
                   jax.ShapeDtypeStruct((D_MODEL, D_RNN), BF16),
                   jax.ShapeDtypeStruct((D_RNN // RNN_BLOCK, RNN_BLOCK, RNN_BLOCK), BF16),
                   jax.ShapeDtypeStruct((D_RNN // RNN_BLOCK, RNN_BLOCK, RNN_BLOCK), BF16),
                   jax.ShapeDtypeStruct((D_RNN, D_MODEL), BF16)],
        scratch_shapes=[pltpu.VMEM((tm, D_MODEL), BF16)],
        compiler_params=pltpu.CompilerParams(
            dimension_semantics=("arbitrary", "arbitrary"), vmem_limit_bytes=VMEM_LIMIT),
        name="recurrent_sample",
    )(x, *prm, prev, h0)


def _kv_kernel(x_ref, ng_ref, wk_ref, wv_ref, cos_ref, slo_ref, shi_ref, k_ref, v_ref):
    h = _rms(x_ref[...], ng_ref[...]).astype(BF16)
    k = _dot(h, wk_ref[...].astype(BF16))
    k_ref[...] = _rope(k, cos_ref[...], slo_ref[...], shi_ref[...])
    v_ref[...] = _dot(h, wv_ref[...].astype(BF16))


def _kv_call(x, ng, w_k, w_v, tables, *, tm):
    n_rows = x.shape[0]
    table_tiles = tables[0].shape[0] // tm
    row = lambda i: (i, 0)
    pos = lambda i: (i % table_tiles, 0)
    fixed = lambda i: (0, 0)
    return pl.pallas_call(
        _kv_kernel,
        grid=(n_rows // tm,),
        in_specs=[
            pl.BlockSpec((tm, D_MODEL), row),
            pl.BlockSpec((1, D_MODEL), fixed),
            pl.BlockSpec((D_MODEL, KV_LANES), fixed),
            pl.BlockSpec((D_MODEL, KV_LANES), fixed),
            pl.BlockSpec((tm, LANES), pos),
            pl.BlockSpec((tm, LANES), pos),
            pl.BlockSpec((tm, LANES), pos),
        ],
        out_specs=[pl.BlockSpec((tm, KV_LANES), row), pl.BlockSpec((tm, KV_LANES), row)],
        out_shape=[jax.ShapeDtypeStruct((n_rows, KV_LANES), F32)] * 2,
        compiler_params=pltpu.CompilerParams(
            dimension_semantics=("arbitrary",), vmem_limit_bytes=VMEM_LIMIT),
        name="shared_kv",
    )(x, ng, w_k, w_v, *tables)


def _scores(q, kp, pair):
    rows = q.shape[0]
    hi_half = lax.broadcasted_iota(jnp.int32, (rows, LANES), 1) >= HEAD_DIM
    blocks, heads, keeps = [], [], []
    for gl in range(4):
        p = gl // 2
        keep = hi_half if p else jnp.logical_not(hi_half)
        qg = q[:, gl * LANES:(gl + 1) * LANES]
        blocks.append(jnp.where(keep, qg, 0.0))
        blocks.append(jnp.where(keep, pltpu.roll(qg, HEAD_DIM, 1), 0.0))
        g = 4 * pair + gl
        heads += [2 * g + p, 2 * g + 1 - p]
        keeps.append(keep)
    qs = jnp.concatenate(blocks, axis=0).astype(BF16)
    s = lax.dot_general(qs, kp, (((1,), (1,)), ((), ())), preferred_element_type=F32)
    return s * (HEAD_DIM ** -0.5), heads, keeps


def _attend(scored, vp, allowed, sink_ref, layer):
    s, heads, keeps = scored
    rows = s.shape[0] // 8
    probs = []
    for bi in range(8):
        sb = jnp.where(allowed, s[bi * rows:(bi + 1) * rows], -jnp.inf)
        sink = sink_ref[layer, heads[bi]]
        m = jnp.maximum(jnp.max(sb, axis=-1, keepdims=True), sink)
        pe = jnp.exp(sb - m)
        den = jnp.sum(pe, axis=-1, keepdims=True) + jnp.exp(sink - m)
        probs.append(pe / den)
    ps = jnp.concatenate(probs, axis=0).astype(BF16)
    o = _dot(ps, vp)
    outs = []
    for gl in range(4):
        o0 = o[(2 * gl) * rows:(2 * gl + 1) * rows]
        o1 = o[(2 * gl + 1) * rows:(2 * gl + 2) * rows]
        outs.append(jnp.where(keeps[gl], o0, pltpu.roll(o1, HEAD_DIM, 1)))
    return jnp.concatenate(outs, axis=1)


def _attn_kernel(*refs, layer, tm, tiles_per_seq, prompt, pps):
    it = iter(refs)
    sink_ref, x_ref, ng_ref, wq_ref, cos_ref, slo_ref, shi_ref, wo_ref = (next(it) for _ in range(8))
    ka_ref, va_ref, kb_ref, vb_ref = (next(it) for _ in range(4))
    out_ref = next(it)
    wq_o, wo_o = (None, None) if prompt else (next(it), next(it))
    h_s = next(it)
    i = pl.program_id(0)
    j = pl.program_id(1)
    if prompt:
        wq, wo = wq_ref, wo_ref[...]
    else:
        wq, wo = wq_ref[...].astype(BF16), wo_ref[...].astype(BF16)
        wq_o[...] = wq
        wo_o[...] = wo

    @pl.when(j == 0)
    def _():
        x = x_ref[...]
        h_s[...] = _rms(x, ng_ref[...]).astype(BF16)
        out_ref[...] = x

    h = h_s[...]
    rope = lambda v: _rope(v, cos_ref[...], slo_ref[...], shi_ref[...])
    blk = ATTN_BLOCK
    if prompt:
        first = (i % tiles_per_seq) == 0
        r = lax.broadcasted_iota(jnp.int32, (blk, 2 * blk), 0)
        kk = lax.broadcasted_iota(jnp.int32, (blk, 2 * blk), 1)
        band = (kk > r) & (kk <= r + blk)
        band0 = band & (kk >= jnp.where(first, blk, 0))
        qs = [_dot(h, wq[:, p * PAIR_W:(p + 1) * PAIR_W]) for p in range(pps)]
        scored, values, masks = [], [], []
        for p in range(pps):
            q = rope(qs[p])
            lanes = slice(p * LANES, (p + 1) * LANES)
            for qb in range(tm // blk):
                cur = slice(qb * blk, (qb + 1) * blk)
                if qb == 0:
                    k_prev, v_prev = kb_ref[:, lanes], vb_ref[:, lanes]
                else:
                    prv = slice((qb - 1) * blk, qb * blk)
                    k_prev, v_prev = ka_ref[prv, lanes], va_ref[prv, lanes]
                masks.append(band0 if qb == 0 else band)
                kp = jnp.concatenate([k_prev, ka_ref[cur, lanes]], axis=0).astype(BF16)
                values.append(
                    jnp.concatenate([v_prev, va_ref[cur, lanes]], axis=0).astype(BF16))
                scored.append(_scores(q[cur], kp, j * pps + p))
        outs = [_attend(sc, vp, allowed, sink_ref, layer)
                for sc, vp, allowed in zip(scored, values, masks)]
        per = tm // blk
        o = jnp.concatenate(
            [jnp.concatenate(outs[p * per:(p + 1) * per], axis=0) for p in range(pps)], axis=1)
    else:
        q = rope(_dot(h, wq))
        n_groups = tm // GROUP_ROWS
        t = lax.broadcasted_iota(jnp.int32, (GROUP_ROWS, 2 * blk), 0) - STATE_ROWS
        kk = lax.broadcasted_iota(jnp.int32, (GROUP_ROWS, 2 * blk), 1)
        new_t = kk - blk - STATE_ROWS
        allowed = ((kk < blk) & (kk > t)) | ((kk >= blk) & (new_t >= 0) & (new_t <= t) &
                                               (kk < blk + GROUP_ROWS))
        allowed = allowed & (t >= 0)
        pad = jnp.zeros((blk - GROUP_ROWS, LANES), F32)
        scored, values = [], []
        for b in range(n_groups):
            rows = slice(b * GROUP_ROWS, (b + 1) * GROUP_ROWS)
            kp = jnp.concatenate([ka_ref[b], kb_ref[rows, :], pad], axis=0).astype(BF16)
            values.append(jnp.concatenate([va_ref[b], vb_ref[rows, :], pad], axis=0).astype(BF16))
            scored.append(_scores(q[rows], kp, j))
        o = jnp.concatenate(
            [_attend(sc, vp, allowed, sink_ref, layer) for sc, vp in zip(scored, values)], axis=0)
    out_ref[...] += _dot(o.astype(BF16), wo)


def _attn_call(x, layer, ng, w_q, w_o, sinks, tables, ka, va, kb, vb, *, tm, tiles_per_seq, prompt):
    n_rows = x.shape[0]
    n_tiles = n_rows // tm
    pps = PAIRS_PER_STEP if prompt else 1
    row = lambda i, j: (i, 0)
    pos = lambda i, j: (i % tiles_per_seq, 0)
    if prompt:
        per = tm // ATTN_BLOCK
        cur_spec = pl.BlockSpec((tm, pps * LANES), lambda i, j: (i, j))
        prev_spec = pl.BlockSpec((ATTN_BLOCK, pps * LANES),
                                 lambda i, j: (jnp.maximum(i * per - 1, 0), j))
        kv_specs = [cur_spec, cur_spec, prev_spec, prev_spec]
    else:
        n_groups = tm // GROUP_ROWS
        cache_spec = pl.BlockSpec((n_groups, ATTN_BLOCK, LANES), lambda i, j: (0, 0, j))
        new_spec = pl.BlockSpec((tm, LANES), lambda i, j: (i, j))
        kv_specs = [cache_spec, cache_spec, new_spec, new_spec]
    wq_copy = pl.BlockSpec((D_MODEL, pps * PAIR_W), lambda i, j: (0, j))
    wo_copy = pl.BlockSpec((pps * PAIR_W, D_MODEL), lambda i, j: (j, 0))
    x_shape = jax.ShapeDtypeStruct((n_rows, D_MODEL), F32)
    if prompt:
        wq_spec, wo_spec = wq_copy, wo_copy
        out_specs, out_shape = pl.BlockSpec((tm, D_MODEL), row), x_shape
    else:
        wq_spec = pl.BlockSpec((None, D_MODEL, PAIR_W), lambda i, j: (layer, 0, j))
        wo_spec = pl.BlockSpec((None, PAIR_W, D_MODEL), lambda i, j: (layer, j, 0))
        out_specs = [pl.BlockSpec((tm, D_MODEL), row), wq_copy, wo_copy]
        out_shape = [x_shape, jax.ShapeDtypeStruct((D_MODEL, N_HEADS * HEAD_DIM), BF16),
                     jax.ShapeDtypeStruct((N_HEADS * HEAD_DIM, D_MODEL), BF16)]
    return pl.pallas_call(
        functools.partial(_attn_kernel, layer=layer, tm=tm, tiles_per_seq=tiles_per_seq,
                          prompt=prompt, pps=pps),
        grid=(n_tiles, N_PAIRS // pps),
        in_specs=[
            pl.BlockSpec(memory_space=pltpu.SMEM),
            pl.BlockSpec((tm, D_MODEL), row),
            pl.BlockSpec((None, 1, D_MODEL), lambda i, j: (layer, 0, 0)),
            wq_spec,
            pl.BlockSpec((tm, LANES), pos),
            pl.BlockSpec((tm, LANES), pos),
            pl.BlockSpec((tm, LANES), pos),
            wo_spec,
        ] + kv_specs,
        out_specs=out_specs,
        out_shape=out_shape,
        scratch_shapes=[pltpu.VMEM((tm, D_MODEL), BF16)],
        compiler_params=pltpu.CompilerParams(
            dimension_semantics=("arbitrary", "arbitrary"), vmem_limit_bytes=VMEM_LIMIT),
        name="attention_prompt" if prompt else "attention_sample",
    )(sinks, x, ng, w_q, *tables, w_o, ka, va, kb, vb)


def _rope_tables(pos):
    half = HEAD_DIM // 2
    freqs = ROPE_THETA ** (-jnp.arange(half, dtype=F32) / half)
    ang = pos.astype(F32)[:, None] * freqs[None, :]
    cos, sin = jnp.cos(ang), jnp.sin(ang)
    zero = jnp.zeros_like(sin)
    cos_t = jnp.concatenate([cos, cos] * 2, axis=1)
    sin_lo = jnp.concatenate([-sin, zero] * 2, axis=1)
    sin_hi = jnp.concatenate([zero, sin] * 2, axis=1)
    return cos_t, sin_lo, sin_hi


def kernel(x_prompt, x_sample, state_rglru_h, state_rglru_conv, state_ffn_conv, cache_window_k,
           cache_window_v, a_norm, a_w_in, a_conv_w, a_conv_b, a_gate_a_w, a_gate_a_b, a_gate_x_w,
           a_gate_x_b, a_lambda, a_w_out, kv_norm, w_k, w_v, b_norm, w_q, sinks, w_o, f_norm,
           f_w_up, f_conv_w, f_conv_b, f_w_down, final_norm):
    bp, s_len, _ = x_prompt.shape
    bs, t_len, _ = x_sample.shape
    depth = f_norm.shape[0]
    n_a = a_norm.shape[0]
    assert t_len == GROUP_ROWS - STATE_ROWS and cache_window_k.shape[1] == ATTN_BLOCK
    tiles_per_seq = s_len // TM_PROMPT
    ts = bs * GROUP_ROWS

    xp = x_prompt.reshape(bp * s_len, D_MODEL)
    xs = jnp.pad(x_sample, ((0, 0), (STATE_ROWS, 0), (0, 0))).reshape(ts, D_MODEL)
    pos_p = jnp.arange(s_len, dtype=jnp.int32)
    pos_s = jnp.tile(PAST_LEN - STATE_ROWS + jnp.arange(GROUP_ROWS, dtype=jnp.int32), bs)
    tab_p = _rope_tables(pos_p)
    tab_s = _rope_tables(pos_s)

    vec = lambda v: v.reshape(v.shape[:-1] + (1, v.shape[-1]))
    a_norm3, a_conv_b3, a_lambda3 = vec(a_norm), vec(a_conv_b), vec(a_lambda)
    a_gate_a_b3, a_gate_x_b3 = vec(a_gate_a_b), vec(a_gate_x_b)
    f_norm3, f_conv_b3, b_norm3 = vec(f_norm), vec(f_conv_b), vec(b_norm)

    def rec_params(w_gate_in, w_u_in, w_a, w_x, w_out):
        return (a_norm3, w_gate_in, w_u_in, a_conv_w, a_conv_b3, w_a, a_gate_a_b3, w_x,
                a_gate_x_b3, a_lambda3, w_out)
    cache_k = cache_window_k.reshape(bs, ATTN_BLOCK, KV_LANES)
    cache_v = cache_window_v.reshape(bs, ATTN_BLOCK, KV_LANES)
    rec_h0 = state_rglru_h[:, :, None, :]
    last = slice(tiles_per_seq - 1, None, tiles_per_seq)

    h_p, h_s, c_p, c_s, f_p, f_s = [], [], [], [], [], []
    for layer in range(depth):
        if layer < n_a:
            prm = rec_params(a_w_in, a_w_in, a_gate_a_w, a_gate_x_w, a_w_out)
            xs, ut, yt, *w_b = _rec_sample_call(xs, layer, prm, state_rglru_conv, rec_h0,
                                                tm=ts)
            c_s.append(ut.reshape(bs, GROUP_ROWS, D_RNN)[:, GROUP_ROWS - 3:])
            h_s.append(yt.reshape(bs, GROUP_ROWS, D_RNN)[:, GROUP_ROWS - 1])
            xp, ut, ht = _rec_prompt_call(xp, layer, rec_params(*w_b), tm=TM_PROMPT,
                                          tiles_per_seq=tiles_per_seq)
            c_p.append(ut[last, SUBLANES - 3:])
            h_p.append(ht[last, 0])
        else:
            j = layer - n_a
            if j == 0:
                k_s, v_s = _kv_call(xs, vec(kv_norm), w_k, w_v, tab_s, tm=ts)
                k_p, v_p = _kv_call(xp, vec(kv_norm), w_k, w_v, tab_p, tm=TM_PROMPT)
            xs, w_q_b, w_o_b = _attn_call(xs, j, b_norm3, w_q, w_o, sinks, tab_s, cache_k, cache_v,
                                          k_s, v_s, tm=ts, tiles_per_seq=1, prompt=False)
            xp = _attn_call(xp, j, b_norm3, w_q_b, w_o_b, sinks, tab_p, k_p, v_p, k_p, v_p,
                            tm=TM_PROMPT, tiles_per_seq=tiles_per_seq, prompt=True)
        fn = vec(final_norm) if layer == depth - 1 else None
        if layer == 0:
            xs, gt, wg_b, wu_b, wd_b = _ffn_call(
                xs, layer, f_norm3, f_w_up, f_w_up, f_conv_w, f_conv_b3, f_w_down, state_ffn_conv,
                fn, tm=ts, carry=False, tiles_per_seq=1, cast_own=True)
            u_cols = 0
        else:
            xs, gt = _ffn_call(xs, layer, f_norm3, wg_b, wu_b, f_conv_w, f_conv_b3, wd_b,
                               state_ffn_conv, fn, tm=ts, carry=False, tiles_per_seq=1,
                               u_cols=u_cols)
        f_s.append(gt.reshape(bs, GROUP_ROWS, D_FF)[:, GROUP_ROWS - 2:])
        cast_next = (f_w_up, f_w_down) if layer + 1 < depth else None
        xp, gt, *w_next = _ffn_call(xp, layer, f_norm3, wg_b, wu_b, f_conv_w, f_conv_b3, wd_b, None,
                                    fn, tm=TM_PROMPT, carry=True, tiles_per_seq=tiles_per_seq,
                                    u_cols=u_cols, cast_next=cast_next)
        f_p.append(gt[last, SUBLANES - 2:])
        if w_next:
            wg_b, wu_b, wd_b, u_cols = w_next[0], w_next[0], w_next[1], D_FF

    y_prompt = xp.reshape(bp, s_len, D_MODEL)
    y_sample = xs.reshape(bs, GROUP_ROWS, D_MODEL)[:, STATE_ROWS:]
    heads = lambda a: a.reshape(a.shape[0], a.shape[1], N_KV_HEADS, HEAD_DIM)
    win_p = min(ATTN_BLOCK, s_len)
    k_p3 = k_p.reshape(bp, s_len, KV_LANES)
    v_p3 = v_p.reshape(bp, s_len, KV_LANES)
    k_new = k_s.reshape(bs, GROUP_ROWS, KV_LANES)[:, STATE_ROWS:]
    v_new = v_s.reshape(bs, GROUP_ROWS, KV_LANES)[:, STATE_ROWS:]
    k_all = jnp.concatenate([cache_k, k_new], axis=1)
    v_all = jnp.concatenate([cache_v, v_new], axis=1)
    return (y_prompt, y_sample, jnp.stack(h_p), jnp.stack(h_s), jnp.stack(c_p), jnp.stack(c_s),
            jnp.stack(f_p), jnp.stack(f_s), heads(k_p3[:, s_len - win_p:]),
            heads(v_p3[:, s_len - win_p:]), heads(k_all[:, t_len:]), heads(v_all[:, t_len:]))
```

```python
import functools

import jax
import jax.numpy as jnp
from jax import lax
from jax.experimental import pallas as pl
from jax.experimental.pallas import tpu as pltpu

F32 = jnp.float32
BF16 = jnp.bfloat16

D_MODEL = 2048
PAST_LEN = 16384
D_RNN = 2048
RNN_BLOCK = 256
LRU_C = 8.0
N_HEADS = 32
HEAD_DIM = 64
N_KV_HEADS = 8
ATTN_BLOCK = 128
ROPE_THETA = 10000.0
D_FF = 3 * D_MODEL
EPS = 1e-6

LANES = 128
SUBLANES = 8
GROUP_ROWS = SUBLANES
STATE_ROWS = 4
KV_LANES = N_KV_HEADS * HEAD_DIM
PAIR_W = 4 * LANES
N_PAIRS = N_KV_HEADS // 2
PAIRS_PER_STEP = 2
VMEM_LIMIT = 56 * 1024 * 1024

TM_PROMPT = 512
SEG = TM_PROMPT // SUBLANES
TC_RNN = 512
TCS_RNN = 256
TC_RNN_SAMPLE = 256
TF_FFN = 1024
TFS_FFN = 512
TF_FFN_SAMPLE = 512


def _rms(x, g):
    ms = jnp.mean(x * x, axis=-1, keepdims=True)
    return x * lax.rsqrt(ms + EPS) * g


def _dot(a, b):
    return jnp.dot(a, b, preferred_element_type=F32)


def _shifted(xe, k, drop):
    return pltpu.roll(xe, k, 0)[drop:]


def _rope(x, cos, sin_lo, sin_hi):
    w = x.shape[1]
    n = w // LANES
    cosw = jnp.concatenate([cos] * n, axis=1)
    lo = jnp.concatenate([sin_lo] * n, axis=1)
    hi = jnp.concatenate([sin_hi] * n, axis=1)
    half = HEAD_DIM // 2
    return x * cosw + pltpu.roll(x, w - half, 1) * lo + pltpu.roll(x, half, 1) * hi


def _state_rows(state):
    g, n, c = state.shape
    padded = jnp.pad(state, ((0, 0), (STATE_ROWS - n, GROUP_ROWS - STATE_ROWS), (0, 0)))
    return padded.reshape(g * GROUP_ROWS, c)


def _after(w, dep):
    rows = 2 * SUBLANES
    zero = jnp.minimum(jnp.abs(dep[0:rows, 0:LANES]), 0.0).astype(BF16)
    parts = []
    for r0 in range(0, w.shape[0], 256):
        head = w[r0:r0 + rows, 0:LANES] + zero
        if w.shape[1] > LANES:
            head = jnp.concatenate([head, w[r0:r0 + rows, LANES:]], axis=1)
        parts += [head, w[r0 + rows:min(r0 + 256, w.shape[0])]]
    return jnp.concatenate(parts, axis=0)


def _cast_jobs(srcs, layer, total_steps, step):
    in_specs, out_specs, out_shape = [], [], []
    for src in srcs:
        _, r, c = src.shape
        if r % (total_steps * 2 * SUBLANES) == 0:
            rps = r // total_steps
            in_specs.append(pl.BlockSpec((None, rps, c), lambda i, j: (layer, step(i, j), 0)))
            out_specs.append(pl.BlockSpec((rps, c), lambda i, j: (step(i, j), 0)))
        else:
            cps = c // total_steps
            assert cps * total_steps == c and cps % LANES == 0
            in_specs.append(pl.BlockSpec((None, r, cps), lambda i, j: (layer, 0, step(i, j))))
            out_specs.append(pl.BlockSpec((r, cps), lambda i, j: (0, step(i, j))))
        out_shape.append(jax.ShapeDtypeStruct((r, c), BF16))
    return in_specs, out_specs, out_shape


def _gate_dots(uc, wa_ref, wx_ref, blocks):
    ucb = uc.astype(BF16)
    ga = jnp.concatenate(
        [_dot(ucb[:, n * RNN_BLOCK:(n + 1) * RNN_BLOCK], wa_ref[bk]) for n, bk in enumerate(blocks)],
        axis=1)
    gx = jnp.concatenate(
        [_dot(ucb[:, n * RNN_BLOCK:(n + 1) * RNN_BLOCK], wx_ref[bk]) for n, bk in enumerate(blocks)],
        axis=1)
    return ga, gx


def _lru_coeffs(ga, gx, ba, bx, lam, uc):
    r = jax.nn.sigmoid(ga + ba)
    ig = jax.nn.sigmoid(gx + bx)
    z = -lam
    softplus = jnp.maximum(z, 0.0) + jnp.log1p(jnp.exp(-jnp.abs(z)))
    log_a = (-LRU_C * r) * softplus
    a = jnp.exp(log_a)
    sq = -jnp.tanh(log_a) * (a * a + 1.0)
    mult = jnp.where(sq > 0.0, sq * lax.rsqrt(sq), 0.0)
    b = mult * ig * uc
    return a, b


def _ffn_kernel(*refs, tm, tf, tfs, n_steps, tiles_per_seq, carry, final, n_cast):
    it = iter(refs)
    x_ref, ng_ref, wg, wu, cw_ref, cb_ref, wd = (next(it) for _ in range(7))
    prev_ref = None if carry else next(it)
    fn_ref = next(it) if final else None
    cast_src = [next(it) for _ in range(n_cast)]
    out_ref, tail_ref = next(it), next(it)
    cast_dst = [next(it) for _ in range(n_cast)]
    h_s = next(it)
    tail_s = next(it) if carry else None
    i = pl.program_id(0)
    j = pl.program_id(1)

    @pl.when(j == 0)
    def _():
        x = x_ref[...]
        h_s[...] = _rms(x, ng_ref[...]).astype(BF16)
        out_ref[...] = x

    if carry:
        @pl.when((i % tiles_per_seq) == 0)
        def _():
            tail_s[j] = jnp.zeros((SUBLANES, tf), F32)
    else:
        row = lax.broadcasted_iota(jnp.int32, (tm, tfs), 0)
        is_token = (row & (GROUP_ROWS - 1)) >= STATE_ROWS
    for src, dst in zip(cast_src, cast_dst):
        dst[...] = src[...].astype(BF16)

    h = h_s[...]
    tail_in = tail_s[j] if carry else None
    tails = []
    for sc in range(tf // tfs):
        cols = slice(sc * tfs, (sc + 1) * tfs)
        g = _dot(h, wg[:, cols])
        u = _dot(h, wu[:, cols])
        if carry:
            ge = jnp.concatenate([tail_in[:, cols], g], axis=0)
            drop = SUBLANES
        else:
            g = jnp.where(is_token, g, _state_rows(prev_ref[:, :, cols]))
            ge = g
            drop = 0
        gc = cb_ref[:, cols] + cw_ref[0:1, cols] * _shifted(ge, 2, drop)
        gc = gc + cw_ref[1:2, cols] * _shifted(ge, 1, drop)
        gc = gc + cw_ref[2:3, cols] * g
        act = (jax.nn.gelu(gc) * u).astype(BF16)
        out_ref[...] += _dot(act, wd[cols, :])
        tails.append(g[tm - SUBLANES:] if carry else g)
    tail = jnp.concatenate(tails, axis=1)
    tail_ref[0] = tail
    if carry:
        tail_s[j] = tail
    if final:
        @pl.when(j == n_steps - 1)
        def _():
            out_ref[...] = _rms(out_ref[...], fn_ref[...])


def _ffn_call(x, layer, ng, w_up, cw, cb, w_down, prev, fn, cast_next=(), *, tm, carry,
              tiles_per_seq):
    n_rows = x.shape[0]
    tf, tfs = (TF_FFN, TFS_FFN) if carry else (TF_FFN_SAMPLE, TF_FFN_SAMPLE)
    n_steps = D_FF // tf
    n_tiles = n_rows // tm
    tail_rows = SUBLANES if carry else tm
    final = fn is not None
    w_specs = [pl.BlockSpec((D_MODEL, tf), lambda i, j: (0, j)),
               pl.BlockSpec((D_MODEL, tf), lambda i, j: (0, j + n_steps)),
               pl.BlockSpec((tf, D_MODEL), lambda i, j: (j, 0))]
    in_specs = [
        pl.BlockSpec((tm, D_MODEL), lambda i, j: (i, 0)),
        pl.BlockSpec((None, 1, D_MODEL), lambda i, j: (layer, 0, 0)),
        w_specs[0],
        w_specs[1],
        pl.BlockSpec((None, 3, tf), lambda i, j: (layer, 0, j)),
        pl.BlockSpec((None, 1, tf), lambda i, j: (layer, 0, j)),
        w_specs[2],
    ]
    args = [x, ng, w_up, w_up, cw, cb, w_down]
    out_specs = [
        pl.BlockSpec((tm, D_MODEL), lambda i, j: (i, 0)),
        pl.BlockSpec((1, tail_rows, tf), lambda i, j: (i, 0, j)),
    ]
    out_shape = [
        jax.ShapeDtypeStruct((n_rows, D_MODEL), F32),
        jax.ShapeDtypeStruct((n_tiles, tail_rows, D_FF), F32),
    ]
    if not carry:
        n_groups, n_state = prev.shape[1:3]
        in_specs.append(pl.BlockSpec((None, n_groups, n_state, tf), lambda i, j: (layer, 0, 0, j)))
        args.append(prev)
    if final:
        in_specs.append(pl.BlockSpec((1, D_MODEL), lambda i, j: (0, 0)))
        args.append(fn)
    cast_in, cast_out, cast_shape = _cast_jobs(cast_next, layer + 1, n_tiles * n_steps,
                                               lambda i, j: i * n_steps + j)
    in_specs += cast_in
    args += list(cast_next)
    out_specs += cast_out
    out_shape += cast_shape
    scratch = [pltpu.VMEM((tm, D_MODEL), BF16)]
    if carry:
        scratch.append(pltpu.VMEM((n_steps, SUBLANES, tf), F32))
    return pl.pallas_call(
        functools.partial(_ffn_kernel, tm=tm, tf=tf, tfs=tfs, n_steps=n_steps,
                          tiles_per_seq=tiles_per_seq, carry=carry, final=final,
                          n_cast=len(cast_next)),
        grid=(n_tiles, n_steps),
        in_specs=in_specs,
        out_specs=out_specs,
        out_shape=out_shape,
        scratch_shapes=scratch,
        compiler_params=pltpu.CompilerParams(
            dimension_semantics=("arbitrary", "arbitrary"), vmem_limit_bytes=VMEM_LIMIT),
        name="conv_ffn_prompt" if carry else "conv_ffn_sample",
    )(*args)


def _rec_weight_specs(tc):
    nb = tc // RNN_BLOCK
    return [
        pl.BlockSpec((D_MODEL, tc), lambda i, j: (0, j)),
        pl.BlockSpec((D_MODEL, tc), lambda i, j: (0, j)),
        pl.BlockSpec((nb, RNN_BLOCK, RNN_BLOCK), lambda i, j: (j, 0, 0)),
        pl.BlockSpec((nb, RNN_BLOCK, RNN_BLOCK), lambda i, j: (j, 0, 0)),
        pl.BlockSpec((tc, D_MODEL), lambda i, j: (j, 0)),
    ]


def _rec_in_specs(layer, tm, tc, n_steps, stacked):
    nb = tc // RNN_BLOCK
    chunk = lambda i, j: (layer, 0, j)
    if stacked:
        w_gate, w_u, w_a, w_x, w_out = [
            pl.BlockSpec((None, D_MODEL, tc), chunk),
            pl.BlockSpec((None, D_MODEL, tc), lambda i, j: (layer, 0, j + n_steps)),
            pl.BlockSpec((None, nb, RNN_BLOCK, RNN_BLOCK), lambda i, j: (layer, j, 0, 0)),
            pl.BlockSpec((None, nb, RNN_BLOCK, RNN_BLOCK), lambda i, j: (layer, j, 0, 0)),
            pl.BlockSpec((None, tc, D_MODEL), lambda i, j: (layer, j, 0)),
        ]
    else:
        w_gate, w_u, w_a, w_x, w_out = _rec_weight_specs(tc)
    return [
        pl.BlockSpec((tm, D_MODEL), lambda i, j: (i, 0)),
        pl.BlockSpec((None, 1, D_MODEL), lambda i, j: (layer, 0, 0)),
        w_gate,
        w_u,
        pl.BlockSpec((None, 4, tc), chunk),
        pl.BlockSpec((None, 1, tc), chunk),
        w_a,
        pl.BlockSpec((None, 1, tc), chunk),
        w_x,
        pl.BlockSpec((None, 1, tc), chunk),
        pl.BlockSpec((None, 1, tc), chunk),
        w_out,
    ]


def _rec_prompt_kernel(x_ref, ng_ref, wgt_ref, wu_ref, cw_ref, cb_ref, wa_ref, ba_ref, wx_ref,
                       bx_ref, lam_ref, wo_ref, up_src, down_src, out_ref, utail_ref, hlast_ref,
                       up_dst, down_dst, h_s, utail_s, hc_s, *, tm, tc, tcs, n_steps,
                       tiles_per_seq):
    seg = tm // SUBLANES
    i = pl.program_id(0)
    j = pl.program_id(1)
    up_dst[...] = up_src[...].astype(BF16)
    down_dst[...] = down_src[...].astype(BF16)

    @pl.when(j == 0)
    def _():
        x = jnp.swapaxes(x_ref[...].reshape(SUBLANES, seg, D_MODEL), 0, 1)
        x = x.reshape(tm, D_MODEL)
        h_s[...] = _rms(x, ng_ref[...]).astype(BF16)
        out_ref[...] = x

    @pl.when((i % tiles_per_seq) == 0)
    def _():
        utail_s[j] = jnp.zeros((SUBLANES, tc), F32)
        hc_s[j] = jnp.zeros((SUBLANES, tc), F32)

    h = h_s[...]
    sub0 = lax.broadcasted_iota(jnp.int32, (SUBLANES, tcs), 0) == 0
    piece = lambda v, k: v[k * SUBLANES:(k + 1) * SUBLANES]
    tail_in = utail_s[j]
    hc_in = hc_s[j]
    sub_cols = [slice(sc * tcs, (sc + 1) * tcs) for sc in range(tc // tcs)]
    u_tails, h_lasts = [], []
    last = SUBLANES - 1
    us = [_dot(h, wu_ref[:, cols]) for cols in sub_cols]
    ucs_all = []
    for sc, cols in enumerate(sub_cols):
        up = [piece(us[sc], k) for k in range(seg)]
        tail = tail_in[:, cols]
        before = {}
        for d in (1, 2, 3):
            rolled = pltpu.roll(up[seg - d], 1, 0)
            before[-d] = jnp.where(sub0, tail[SUBLANES - d:SUBLANES - d + 1], rolled)
        at = lambda k: up[k] if k >= 0 else before[k]
        cw = cw_ref[:, cols]
        cb = cb_ref[:, cols]
        ucs = []
        for k in range(seg):
            v = cb + cw[0:1] * at(k - 3)
            v = v + cw[1:2] * at(k - 2)
            v = v + cw[2:3] * at(k - 1)
            ucs.append(v + cw[3:4] * up[k])
        ucs_all.append(jnp.concatenate(ucs, axis=0))
        u_tails.append(jnp.concatenate(
            [jnp.zeros((SUBLANES - 3, tcs), F32)] + [up[seg - d][last:last + 1] for d in (3, 2, 1)],
            axis=0))
    projected = []
    for sc, cols in enumerate(sub_cols):
        blocks = [sc * (tcs // RNN_BLOCK) + n for n in range(tcs // RNN_BLOCK)]
        ga, gx = _gate_dots(ucs_all[sc], wa_ref, wx_ref, blocks)
        projected.append((ga, gx, _dot(h, _after(wgt_ref[:, cols], ga))))
    acts = []
    for sc, cols in enumerate(sub_cols):
        ga, gx, gate = projected[sc]
        a, b = _lru_coeffs(ga, gx, ba_ref[:, cols], bx_ref[:, cols], lam_ref[:, cols],
                           ucs_all[sc])
        a_cum, b_cum = [piece(a, 0)], [piece(b, 0)]
        for k in range(1, seg):
            ak = piece(a, k)
            b_cum.append(ak * b_cum[-1] + piece(b, k))
            a_cum.append(ak * a_cum[-1])
        a_seg, b_seg = a_cum[-1], b_cum[-1]
        c = hc_in[0:1, cols]
        carries = [c]
        for s in range(SUBLANES):
            c = a_seg[s:s + 1] * c + b_seg[s:s + 1]
            carries.append(c)
        carry_in = jnp.concatenate(carries[:SUBLANES], axis=0)
        y = jnp.concatenate([a_cum[k] * carry_in + b_cum[k] for k in range(seg)], axis=0)
        acts.append((jax.nn.gelu(gate) * y).astype(BF16))
        h_lasts.append(jnp.broadcast_to(carries[SUBLANES], (SUBLANES, tcs)))
    for sc, cols in enumerate(sub_cols):
        out_ref[...] += _dot(acts[sc], wo_ref[cols, :])
    h_last = jnp.concatenate(h_lasts, axis=1)
    u_tail = jnp.concatenate(u_tails, axis=1)
    hc_s[j] = h_last
    hlast_ref[0] = h_last
    utail_s[j] = u_tail
    utail_ref[0] = u_tail

    @pl.when(j == n_steps - 1)
    def _():
        acc = jnp.swapaxes(out_ref[...].reshape(seg, SUBLANES, D_MODEL), 0, 1)
        out_ref[...] = acc.reshape(tm, D_MODEL)


def _rec_prompt_call(x, layer, prm, ffn_weights, *, tm, tiles_per_seq):
    n_rows = x.shape[0]
    tc, tcs = TC_RNN, TCS_RNN
    n_steps = D_RNN // tc
    n_tiles = n_rows // tm
    tail_spec = pl.BlockSpec((1, SUBLANES, tc), lambda i, j: (i, 0, j))
    tail_shape = jax.ShapeDtypeStruct((n_tiles, SUBLANES, D_RNN), F32)
    cast_in, cast_out, cast_shape = _cast_jobs(ffn_weights, layer, n_tiles * n_steps,
                                               lambda i, j: i * n_steps + j)
    return pl.pallas_call(
        functools.partial(_rec_prompt_kernel, tm=tm, tc=tc, tcs=tcs, n_steps=n_steps,
                          tiles_per_seq=tiles_per_seq),
        grid=(n_tiles, n_steps),
        in_specs=_rec_in_specs(layer, tm, tc, n_steps, stacked=False) + cast_in,
        out_specs=[pl.BlockSpec((tm, D_MODEL), lambda i, j: (i, 0)), tail_spec, tail_spec]
        + cast_out,
        out_shape=[jax.ShapeDtypeStruct((n_rows, D_MODEL), F32), tail_shape, tail_shape]
        + cast_shape,
        scratch_shapes=[pltpu.VMEM((tm, D_MODEL), BF16),
                        pltpu.VMEM((n_steps, SUBLANES, tc), F32),
                        pltpu.VMEM((n_steps, SUBLANES, tc), F32)],
        compiler_params=pltpu.CompilerParams(
            dimension_semantics=("arbitrary", "arbitrary"), vmem_limit_bytes=VMEM_LIMIT),
        name="recurrent_prompt",
    )(x, *prm, *ffn_weights)


def _rec_sample_kernel(x_ref, ng_ref, wgt_ref, wu_ref, cw_ref, cb_ref, wa_ref, ba_ref, wx_ref,
                       bx_ref, lam_ref, wo_ref, prev_ref, h0_ref, out_ref, utail_ref, ytail_ref,
                       wgt_o, wu_o, wa_o, wx_o, wo_o, h_s, *, tm, tc):
    j = pl.program_id(1)
    wgt, wu, wa, wx, wo = (w[...].astype(BF16) for w in (wgt_ref, wu_ref, wa_ref, wx_ref, wo_ref))
    for o_ref, w in zip((wgt_o, wu_o, wa_o, wx_o, wo_o), (wgt, wu, wa, wx, wo)):
        o_ref[...] = w

    @pl.when(j == 0)
    def _():
        x = x_ref[...]
        h_s[...] = _rms(x, ng_ref[...]).astype(BF16)
        out_ref[...] = x

    h = h_s[...]
    gate = _dot(h, wgt)
    u = _dot(h, wu)
    row8 = lax.broadcasted_iota(jnp.int32, (tm, tc), 0) & (GROUP_ROWS - 1)
    is_token = row8 >= STATE_ROWS
    u = jnp.where(is_token, u, _state_rows(prev_ref[...]))
    cw = cw_ref[...]
    uc = cb_ref[...] + cw[0:1] * _shifted(u, 3, 0)
    uc = uc + cw[1:2] * _shifted(u, 2, 0)
    uc = uc + cw[2:3] * _shifted(u, 1, 0)
    uc = uc + cw[3:4] * u
    ga, gx = _gate_dots(uc, wa, wx, list(range(tc // RNN_BLOCK)))
    a, b = _lru_coeffs(ga, gx, ba_ref[...], bx_ref[...], lam_ref[...], uc)
    a = jnp.where(is_token, a, 0.0)
    b = jnp.where(is_token, b, _state_rows(h0_ref[...]))
    for d in (1, 2, 4):
        m = row8 >= d
        a_sh = pltpu.roll(a, d, 0)
        b_sh = pltpu.roll(b, d, 0)
        b = jnp.where(m, a * b_sh + b, b)
        a = jnp.where(m, a * a_sh, a)
    act = (jax.nn.gelu(gate) * b).astype(BF16)
    out_ref[...] += _dot(act, wo)
    utail_ref[...] = u
    ytail_ref[...] = b


def _rec_sample_call(x, layer, prm, prev, h0, *, tm):
    n_rows = x.shape[0]
    tc = TC_RNN_SAMPLE
    n_steps = D_RNN // tc
    state_spec = pl.BlockSpec((tm, tc), lambda i, j: (i, j))
    state_shape = jax.ShapeDtypeStruct((n_rows, D_RNN), F32)
    return pl.pallas_call(
        functools.partial(_rec_sample_kernel, tm=tm, tc=tc),
        grid=(n_rows // tm, n_steps),
        in_specs=_rec_in_specs(layer, tm, tc, n_steps, stacked=True) + [
            pl.BlockSpec((None,) + prev.shape[1:3] + (tc,), lambda i, j: (layer, 0, 0, j)),
            pl.BlockSpec((None,) + h0.shape[1:3] + (tc,), lambda i, j: (layer, 0, 0, j))],
        out_specs=[pl.BlockSpec((tm, D_MODEL), lambda i, j: (i, 0)), state_spec, state_spec]
        + _rec_weight_specs(tc),
        out_shape=[jax.ShapeDtypeStruct((n_rows, D_MODEL), F32), state_shape, state_shape,
                   jax.ShapeDtypeStruct((D_MODEL, D_RNN), BF16),
                   jax.ShapeDtypeStruct((D_MODEL, D_RNN), BF16),
                   jax.ShapeDtypeStruct((D_RNN // RNN_BLOCK, RNN_BLOCK, RNN_BLOCK), BF16),
                   jax.ShapeDtypeStruct((D_RNN // RNN_BLOCK, RNN_BLOCK, RNN_BLOCK), BF16),
                   jax.ShapeDtypeStruct((D_RNN, D_MODEL), BF16)],
        scratch_shapes=[pltpu.VMEM((tm, D_MODEL), BF16)],
        compiler_params=pltpu.CompilerParams(
            dimension_semantics=("arbitrary", "arbitrary"), vmem_limit_bytes=VMEM_LIMIT),
        name="recurrent_sample",
    )(x, *prm, prev, h0)


def _kv_kernel(x_ref, ng_ref, wk_ref, wv_ref, cos_ref, slo_ref, shi_ref, k_ref, v_ref):
    h = _rms(x_ref[...], ng_ref[...]).astype(BF16)
    k = _dot(h, wk_ref[...].astype(BF16))
    k_ref[...] = _rope(k, cos_ref[...], slo_ref[...], shi_ref[...])
    v_ref[...] = _dot(h, wv_ref[...].astype(BF16))


def _kv_call(x, ng, w_k, w_v, tables, *, tm):
    n_rows = x.shape[0]
    table_tiles = tables[0].shape[0] // tm
    row = lambda i: (i, 0)
    pos = lambda i: (i % table_tiles, 0)
    fixed = lambda i: (0, 0)
    return pl.pallas_call(
        _kv_kernel,
        grid=(n_rows // tm,),
        in_specs=[
            pl.BlockSpec((tm, D_MODEL), row),
            pl.BlockSpec((1, D_MODEL), fixed),
            pl.BlockSpec((D_MODEL, KV_LANES), fixed),
            pl.BlockSpec((D_MODEL, KV_LANES), fixed),
            pl.BlockSpec((tm, LANES), pos),
            pl.BlockSpec((tm, LANES), pos),
            pl.BlockSpec((tm, LANES), pos),
        ],
        out_specs=[pl.BlockSpec((tm, KV_LANES), row), pl.BlockSpec((tm, KV_LANES), row)],
        out_shape=[jax.ShapeDtypeStruct((n_rows, KV_LANES), F32)] * 2,
        compiler_params=pltpu.CompilerParams(
            dimension_semantics=("arbitrary",), vmem_limit_bytes=VMEM_LIMIT),
        name="shared_kv",
    )(x, ng, w_k, w_v, *tables)


def _scores(q, kp, pair):
    rows = q.shape[0]
    hi_half = lax.broadcasted_iota(jnp.int32, (rows, LANES), 1) >= HEAD_DIM
    blocks, heads, keeps = [], [], []
    for gl in range(4):
        p = gl // 2
        keep = hi_half if p else jnp.logical_not(hi_half)
        qg = q[:, gl * LANES:(gl + 1) * LANES]
        blocks.append(jnp.where(keep, qg, 0.0))
        blocks.append(jnp.where(keep, pltpu.roll(qg, HEAD_DIM, 1), 0.0))
        g = 4 * pair + gl
        heads += [2 * g + p, 2 * g + 1 - p]
        keeps.append(keep)
    qs = jnp.concatenate(blocks, axis=0).astype(BF16)
    s = lax.dot_general(qs, kp, (((1,), (1,)), ((), ())), preferred_element_type=F32)
    return s * (HEAD_DIM ** -0.5), heads, keeps


def _attend(scored, vp, allowed, sink_ref, layer):
    s, heads, keeps = scored
    rows = s.shape[0] // 8
    probs = []
    for bi in range(8):
        sb = jnp.where(allowed, s[bi * rows:(bi + 1) * rows], -jnp.inf)
        sink = sink_ref[layer, heads[bi]]
        m = jnp.maximum(jnp.max(sb, axis=-1, keepdims=True), sink)
        pe = jnp.exp(sb - m)
        den = jnp.sum(pe, axis=-1, keepdims=True) + jnp.exp(sink - m)
        probs.append(pe / den)
    ps = jnp.concatenate(probs, axis=0).astype(BF16)
    o = _dot(ps, vp)
    outs = []
    for gl in range(4):
        o0 = o[(2 * gl) * rows:(2 * gl + 1) * rows]
        o1 = o[(2 * gl + 1) * rows:(2 * gl + 2) * rows]
        outs.append(jnp.where(keeps[gl], o0, pltpu.roll(o1, HEAD_DIM, 1)))
    return jnp.concatenate(outs, axis=1)


def _attn_kernel(*refs, layer, tm, tiles_per_seq, prompt, pps):
    it = iter(refs)
    sink_ref, x_ref, ng_ref, wq_ref, cos_ref, slo_ref, shi_ref, wo_ref = (next(it) for _ in range(8))
    ka_ref, va_ref, kb_ref, vb_ref = (next(it) for _ in range(4))
    out_ref = next(it)
    wq_o, wo_o = (None, None) if prompt else (next(it), next(it))
    h_s = next(it)
    i = pl.program_id(0)
    j = pl.program_id(1)
    if prompt:
        wq, wo = wq_ref, wo_ref[...]
    else:
        wq, wo = wq_ref[...].astype(BF16), wo_ref[...].astype(BF16)
        wq_o[...] = wq
        wo_o[...] = wo

    @pl.when(j == 0)
    def _():
        x = x_ref[...]
        h_s[...] = _rms(x, ng_ref[...]).astype(BF16)
        out_ref[...] = x

    h = h_s[...]
    rope = lambda v: _rope(v, cos_ref[...], slo_ref[...], shi_ref[...])
    blk = ATTN_BLOCK
    if prompt:
        first = (i % tiles_per_seq) == 0
        r = lax.broadcasted_iota(jnp.int32, (blk, 2 * blk), 0)
        kk = lax.broadcasted_iota(jnp.int32, (blk, 2 * blk), 1)
        band = (kk > r) & (kk <= r + blk)
        band0 = band & (kk >= jnp.where(first, blk, 0))
        qs = [_dot(h, wq[:, p * PAIR_W:(p + 1) * PAIR_W]) for p in range(pps)]
        scored, values, masks = [], [], []
        for p in range(pps):
            q = rope(qs[p])
            lanes = slice(p * LANES, (p + 1) * LANES)
            for qb in range(tm // blk):
                cur = slice(qb * blk, (qb + 1) * blk)
                if qb == 0:
                    k_prev, v_prev = kb_ref[:, lanes], vb_ref[:, lanes]
                else:
                    prv = slice((qb - 1) * blk, qb * blk)
                    k_prev, v_prev = ka_ref[prv, lanes], va_ref[prv, lanes]
                masks.append(band0 if qb == 0 else band)
                kp = jnp.concatenate([k_prev, ka_ref[cur, lanes]], axis=0).astype(BF16)
                values.append(
                    jnp.concatenate([v_prev, va_ref[cur, lanes]], axis=0).astype(BF16))
                scored.append(_scores(q[cur], kp, j * pps + p))
        outs = [_attend(sc, vp, allowed, sink_ref, layer)
                for sc, vp, allowed in zip(scored, values, masks)]
        per = tm // blk
        o = jnp.concatenate(
            [jnp.concatenate(outs[p * per:(p + 1) * per], axis=0) for p in range(pps)], axis=1)
    else:
        q = rope(_dot(h, wq))
        n_groups = tm // GROUP_ROWS
        t = lax.broadcasted_iota(jnp.int32, (GROUP_ROWS, 2 * blk), 0) - STATE_ROWS
        kk = lax.broadcasted_iota(jnp.int32, (GROUP_ROWS, 2 * blk), 1)
        new_t = kk - blk - STATE_ROWS
        allowed = ((kk < blk) & (kk > t)) | ((kk >= blk) & (new_t >= 0) & (new_t <= t) &
                                               (kk < blk + GROUP_ROWS))
        allowed = allowed & (t >= 0)
        pad = jnp.zeros((blk - GROUP_ROWS, LANES), F32)
        scored, values = [], []
        for b in range(n_groups):
            rows = slice(b * GROUP_ROWS, (b + 1) * GROUP_ROWS)
            kp = jnp.concatenate([ka_ref[b], kb_ref[rows, :], pad], axis=0).astype(BF16)
            values.append(jnp.concatenate([va_ref[b], vb_ref[rows, :], pad], axis=0).astype(BF16))
            scored.append(_scores(q[rows], kp, j))
        o = jnp.concatenate(
            [_attend(sc, vp, allowed, sink_ref, layer) for sc, vp in zip(scored, values)], axis=0)
    out_ref[...] += _dot(o.astype(BF16), wo)


def _attn_call(x, layer, ng, w_q, w_o, sinks, tables, ka, va, kb, vb, *, tm, tiles_per_seq, prompt):
    n_rows = x.shape[0]
    n_tiles = n_rows // tm
    pps = PAIRS_PER_STEP if prompt else 1
    row = lambda i, j: (i, 0)
    pos = lambda i, j: (i % tiles_per_seq, 0)
    if prompt:
        per = tm // ATTN_BLOCK
        cur_spec = pl.BlockSpec((tm, pps * LANES), lambda i, j: (i, j))
        prev_spec = pl.BlockSpec((ATTN_BLOCK, pps * LANES),
                                 lambda i, j: (jnp.maximum(i * per - 1, 0), j))
        kv_specs = [cur_spec, cur_spec, prev_spec, prev_spec]
    else:
        n_groups = tm // GROUP_ROWS
        cache_spec = pl.BlockSpec((n_groups, ATTN_BLOCK, LANES), lambda i, j: (0, 0, j))
        new_spec = pl.BlockSpec((tm, LANES), lambda i, j: (i, j))
        kv_specs = [cache_spec, cache_spec, new_spec, new_spec]
    wq_copy = pl.BlockSpec((D_MODEL, pps * PAIR_W), lambda i, j: (0, j))
    wo_copy = pl.BlockSpec((pps * PAIR_W, D_MODEL), lambda i, j: (j, 0))
    x_shape = jax.ShapeDtypeStruct((n_rows, D_MODEL), F32)
    if prompt:
        wq_spec, wo_spec = wq_copy, wo_copy
        out_specs, out_shape = pl.BlockSpec((tm, D_MODEL), row), x_shape
    else:
        wq_spec = pl.BlockSpec((None, D_MODEL, PAIR_W), lambda i, j: (layer, 0, j))
        wo_spec = pl.BlockSpec((None, PAIR_W, D_MODEL), lambda i, j: (layer, j, 0))
        out_specs = [pl.BlockSpec((tm, D_MODEL), row), wq_copy, wo_copy]
        out_shape = [x_shape, jax.ShapeDtypeStruct((D_MODEL, N_HEADS * HEAD_DIM), BF16),
                     jax.ShapeDtypeStruct((N_HEADS * HEAD_DIM, D_MODEL), BF16)]
    return pl.pallas_call(
        functools.partial(_attn_kernel, layer=layer, tm=tm, tiles_per_seq=tiles_per_seq,
                          prompt=prompt, pps=pps),
        grid=(n_tiles, N_PAIRS // pps),
        in_specs=[
            pl.BlockSpec(memory_space=pltpu.SMEM),
            pl.BlockSpec((tm, D_MODEL), row),
            pl.BlockSpec((None, 1, D_MODEL), lambda i, j: (layer, 0, 0)),
            wq_spec,
            pl.BlockSpec((tm, LANES), pos),
            pl.BlockSpec((tm, LANES), pos),
            pl.BlockSpec((tm, LANES), pos),
            wo_spec,
        ] + kv_specs,
        out_specs=out_specs,
        out_shape=out_shape,
        scratch_shapes=[pltpu.VMEM((tm, D_MODEL), BF16)],
        compiler_params=pltpu.CompilerParams(
            dimension_semantics=("arbitrary", "arbitrary"), vmem_limit_bytes=VMEM_LIMIT),
        name="attention_prompt" if prompt else "attention_sample",
    )(sinks, x, ng, w_q, *tables, w_o, ka, va, kb, vb)


def _rope_tables(pos):
    half = HEAD_DIM // 2
    freqs = ROPE_THETA ** (-jnp.arange(half, dtype=F32) / half)
    ang = pos.astype(F32)[:, None] * freqs[None, :]
    cos, sin = jnp.cos(ang), jnp.sin(ang)
    zero = jnp.zeros_like(sin)
    cos_t = jnp.concatenate([cos, cos] * 2, axis=1)
    sin_lo = jnp.concatenate([-sin, zero] * 2, axis=1)
    sin_hi = jnp.concatenate([zero, sin] * 2, axis=1)
    return cos_t, sin_lo, sin_hi


def kernel(x_prompt, x_sample, state_rglru_h, state_rglru_conv, state_ffn_conv, cache_window_k,
           cache_window_v, a_norm, a_w_in, a_conv_w, a_conv_b, a_gate_a_w, a_gate_a_b, a_gate_x_w,
           a_gate_x_b, a_lambda, a_w_out, kv_norm, w_k, w_v, b_norm, w_q, sinks, w_o, f_norm,
           f_w_up, f_conv_w, f_conv_b, f_w_down, final_norm):
    bp, s_len, _ = x_prompt.shape
    bs, t_len, _ = x_sample.shape
    depth = f_norm.shape[0]
    n_a = a_norm.shape[0]
    assert t_len == GROUP_ROWS - STATE_ROWS and cache_window_k.shape[1] == ATTN_BLOCK
    tiles_per_seq = s_len // TM_PROMPT
    ts = bs * GROUP_ROWS

    xp = x_prompt.reshape(bp * s_len, D_MODEL)
    xs = jnp.pad(x_sample, ((0, 0), (STATE_ROWS, 0), (0, 0))).reshape(ts, D_MODEL)
    pos_p = jnp.arange(s_len, dtype=jnp.int32)
    pos_s = jnp.tile(PAST_LEN - STATE_ROWS + jnp.arange(GROUP_ROWS, dtype=jnp.int32), bs)
    tab_p = _rope_tables(pos_p)
    tab_s = _rope_tables(pos_s)

    vec = lambda v: v.reshape(v.shape[:-1] + (1, v.shape[-1]))
    a_norm3, a_conv_b3, a_lambda3 = vec(a_norm), vec(a_conv_b), vec(a_lambda)
    a_gate_a_b3, a_gate_x_b3 = vec(a_gate_a_b), vec(a_gate_x_b)
    f_norm3, f_conv_b3, b_norm3 = vec(f_norm), vec(f_conv_b), vec(b_norm)

    def rec_params(w_gate_in, w_u_in, w_a, w_x, w_out):
        return (a_norm3, w_gate_in, w_u_in, a_conv_w, a_conv_b3, w_a, a_gate_a_b3, w_x,
                a_gate_x_b3, a_lambda3, w_out)
    cache_k = cache_window_k.reshape(bs, ATTN_BLOCK, KV_LANES)
    cache_v = cache_window_v.reshape(bs, ATTN_BLOCK, KV_LANES)
    rec_h0 = state_rglru_h[:, :, None, :]
    last = slice(tiles_per_seq - 1, None, tiles_per_seq)

    h_p, h_s, c_p, c_s, f_p, f_s = [], [], [], [], [], []
    for layer in range(depth):
        if layer < n_a:
            prm = rec_params(a_w_in, a_w_in, a_gate_a_w, a_gate_x_w, a_w_out)
            xs, ut, yt, *w_b = _rec_sample_call(xs, layer, prm, state_rglru_conv, rec_h0,
                                                tm=ts)
            c_s.append(ut.reshape(bs, GROUP_ROWS, D_RNN)[:, GROUP_ROWS - 3:])
            h_s.append(yt.reshape(bs, GROUP_ROWS, D_RNN)[:, GROUP_ROWS - 1])
            xp, ut, ht, w_up_b, w_down_b = _rec_prompt_call(
                xp, layer, rec_params(*w_b), (f_w_up, f_w_down), tm=TM_PROMPT,
                tiles_per_seq=tiles_per_seq)
            c_p.append(ut[last, SUBLANES - 3:])
            h_p.append(ht[last, 0])
        else:
            j = layer - n_a
            if j == 0:
                k_s, v_s = _kv_call(xs, vec(kv_norm), w_k, w_v, tab_s, tm=ts)
                k_p, v_p = _kv_call(xp, vec(kv_norm), w_k, w_v, tab_p, tm=TM_PROMPT)
            xs, w_q_b, w_o_b = _attn_call(xs, j, b_norm3, w_q, w_o, sinks, tab_s, cache_k, cache_v,
                                          k_s, v_s, tm=ts, tiles_per_seq=1, prompt=False)
            xp = _attn_call(xp, j, b_norm3, w_q_b, w_o_b, sinks, tab_p, k_p, v_p, k_p, v_p,
                            tm=TM_PROMPT, tiles_per_seq=tiles_per_seq, prompt=True)
        fn = vec(final_norm) if layer == depth - 1 else None
        xs, gt = _ffn_call(xs, layer, f_norm3, w_up_b, f_conv_w, f_conv_b3, w_down_b,
                           state_ffn_conv, fn, tm=ts, carry=False, tiles_per_seq=1)
        f_s.append(gt.reshape(bs, GROUP_ROWS, D_FF)[:, GROUP_ROWS - 2:])
        cast_next = (f_w_up, f_w_down) if n_a <= layer + 1 < depth else ()
        xp, gt, *w_next = _ffn_call(xp, layer, f_norm3, w_up_b, f_conv_w, f_conv_b3, w_down_b, None,
                                    fn, cast_next, tm=TM_PROMPT, carry=True,
                                    tiles_per_seq=tiles_per_seq)
        f_p.append(gt[last, SUBLANES - 2:])
        if w_next:
            w_up_b, w_down_b = w_next

    y_prompt = xp.reshape(bp, s_len, D_MODEL)
    y_sample = xs.reshape(bs, GROUP_ROWS, D_MODEL)[:, STATE_ROWS:]
    heads = lambda a: a.reshape(a.shape[0], a.shape[1], N_KV_HEADS, HEAD_DIM)
    win_p = min(ATTN_BLOCK, s_len)
    k_p3 = k_p.reshape(bp, s_len, KV_LANES)
    v_p3 = v_p.reshape(bp, s_len, KV_LANES)
    k_new = k_s.reshape(bs, GROUP_ROWS, KV_LANES)[:, STATE_ROWS:]
    v_new = v_s.reshape(bs, GROUP_ROWS, KV_LANES)[:, STATE_ROWS:]
    win_k = jnp.concatenate([cache_k[:, t_len:], k_new], axis=1)
    win_v = jnp.concatenate([cache_v[:, t_len:], v_new], axis=1)
    return (y_prompt, y_sample, jnp.stack(h_p), jnp.stack(h_s), jnp.stack(c_p), jnp.stack(c_s),
            jnp.stack(f_p), jnp.stack(f_s), heads(k_p3[:, s_len - win_p:]),
            heads(v_p3[:, s_len - win_p:]), heads(win_k), heads(win_v))
```

```python
import functools

import jax
import jax.numpy as jnp
from jax import lax
from jax.experimental import pallas as pl
from jax.experimental.pallas import tpu as pltpu

F32 = jnp.float32
BF16 = jnp.bfloat16

D_MODEL = 2048
PAST_LEN = 16384
D_RNN = 2048
RNN_BLOCK = 256
LRU_C = 8.0
N_HEADS = 32
HEAD_DIM = 64
N_KV_HEADS = 8
ATTN_BLOCK = 128
ROPE_THETA = 10000.0
D_FF = 3 * D_MODEL
EPS = 1e-6

LANES = 128
SUBLANES = 8
GROUP_ROWS = SUBLANES
STATE_ROWS = 4
KV_LANES = N_KV_HEADS * HEAD_DIM
PAIR_W = 4 * LANES
N_PAIRS = N_KV_HEADS // 2
PAIRS_PER_STEP = 2
VMEM_LIMIT = 56 * 1024 * 1024

TM_PROMPT = 512
SEG = TM_PROMPT // SUBLANES
TC_RNN = 512
TCS_RNN = 256
TC_RNN_SAMPLE = 256
TF_FFN = 1024
TFS_FFN = 512
TF_FFN_SAMPLE = 1024


def _rms(x, g):
    ms = jnp.mean(x * x, axis=-1, keepdims=True)
    return x * lax.rsqrt(ms + EPS) * g


def _dot(a, b):
    return jnp.dot(a, b, preferred_element_type=F32)


def _shifted(xe, k, drop):
    return pltpu.roll(xe, k, 0)[drop:]


def _rope(x, cos, sin_lo, sin_hi):
    w = x.shape[1]
    n = w // LANES
    cosw = jnp.concatenate([cos] * n, axis=1)
    lo = jnp.concatenate([sin_lo] * n, axis=1)
    hi = jnp.concatenate([sin_hi] * n, axis=1)
    half = HEAD_DIM // 2
    return x * cosw + pltpu.roll(x, w - half, 1) * lo + pltpu.roll(x, half, 1) * hi


def _state_rows(state):
    g, n, c = state.shape
    padded = jnp.pad(state, ((0, 0), (STATE_ROWS - n, GROUP_ROWS - STATE_ROWS), (0, 0)))
    return padded.reshape(g * GROUP_ROWS, c)


def _after(w, dep):
    rows = 2 * SUBLANES
    zero = jnp.minimum(jnp.abs(dep[0:rows, 0:LANES]), 0.0).astype(BF16)
    parts = []
    for r0 in range(0, w.shape[0], 256):
        head = w[r0:r0 + rows, 0:LANES] + zero
        if w.shape[1] > LANES:
            head = jnp.concatenate([head, w[r0:r0 + rows, LANES:]], axis=1)
        parts += [head, w[r0 + rows:min(r0 + 256, w.shape[0])]]
    return jnp.concatenate(parts, axis=0)


def _cast_jobs(srcs, layer, total_steps, step):
    in_specs, out_specs, out_shape = [], [], []
    for src in srcs:
        _, r, c = src.shape
        if r % (total_steps * 2 * SUBLANES) == 0:
            rps = r // total_steps
            in_specs.append(pl.BlockSpec((None, rps, c), lambda i, j: (layer, step(i, j), 0)))
            out_specs.append(pl.BlockSpec((rps, c), lambda i, j: (step(i, j), 0)))
        else:
            cps = c // total_steps
            assert cps * total_steps == c and cps % LANES == 0
            in_specs.append(pl.BlockSpec((None, r, cps), lambda i, j: (layer, 0, step(i, j))))
            out_specs.append(pl.BlockSpec((r, cps), lambda i, j: (0, step(i, j))))
        out_shape.append(jax.ShapeDtypeStruct((r, c), BF16))
    return in_specs, out_specs, out_shape


def _gate_dots(uc, wa_ref, wx_ref, blocks):
    ucb = uc.astype(BF16)
    ga = jnp.concatenate(
        [_dot(ucb[:, n * RNN_BLOCK:(n + 1) * RNN_BLOCK], wa_ref[bk]) for n, bk in enumerate(blocks)],
        axis=1)
    gx = jnp.concatenate(
        [_dot(ucb[:, n * RNN_BLOCK:(n + 1) * RNN_BLOCK], wx_ref[bk]) for n, bk in enumerate(blocks)],
        axis=1)
    return ga, gx


def _lru_coeffs(ga, gx, ba, bx, lam, uc):
    r = jax.nn.sigmoid(ga + ba)
    ig = jax.nn.sigmoid(gx + bx)
    z = -lam
    softplus = jnp.maximum(z, 0.0) + jnp.log1p(jnp.exp(-jnp.abs(z)))
    log_a = (-LRU_C * r) * softplus
    a = jnp.exp(log_a)
    sq = -jnp.tanh(log_a) * (a * a + 1.0)
    mult = jnp.where(sq > 0.0, sq * lax.rsqrt(sq), 0.0)
    b = mult * ig * uc
    return a, b


def _ffn_kernel(*refs, tm, tf, tfs, n_steps, tiles_per_seq, carry, final, n_cast):
    it = iter(refs)
    x_ref, ng_ref, wg, wu, cw_ref, cb_ref, wd = (next(it) for _ in range(7))
    prev_ref = None if carry else next(it)
    fn_ref = next(it) if final else None
    cast_src = [next(it) for _ in range(n_cast)]
    out_ref, tail_ref = next(it), next(it)
    cast_dst = [next(it) for _ in range(n_cast)]
    h_s = next(it)
    tail_s = next(it) if carry else None
    i = pl.program_id(0)
    j = pl.program_id(1)

    @pl.when(j == 0)
    def _():
        x = x_ref[...]
        h_s[...] = _rms(x, ng_ref[...]).astype(BF16)
        out_ref[...] = x

    if carry:
        @pl.when((i % tiles_per_seq) == 0)
        def _():
            tail_s[j] = jnp.zeros((SUBLANES, tf), F32)
    else:
        row = lax.broadcasted_iota(jnp.int32, (tm, tfs), 0)
        is_token = (row & (GROUP_ROWS - 1)) >= STATE_ROWS
    for src, dst in zip(cast_src, cast_dst):
        dst[...] = src[...].astype(BF16)

    h = h_s[...]
    tail_in = tail_s[j] if carry else None
    tails = []
    for sc in range(tf // tfs):
        cols = slice(sc * tfs, (sc + 1) * tfs)
        g = _dot(h, wg[:, cols])
        u = _dot(h, wu[:, cols])
        if carry:
            ge = jnp.concatenate([tail_in[:, cols], g], axis=0)
            drop = SUBLANES
        else:
            g = jnp.where(is_token, g, _state_rows(prev_ref[:, :, cols]))
            ge = g
            drop = 0
        gc = cb_ref[:, cols] + cw_ref[0:1, cols] * _shifted(ge, 2, drop)
        gc = gc + cw_ref[1:2, cols] * _shifted(ge, 1, drop)
        gc = gc + cw_ref[2:3, cols] * g
        act = (jax.nn.gelu(gc) * u).astype(BF16)
        out_ref[...] += _dot(act, wd[cols, :])
        tails.append(g[tm - SUBLANES:] if carry else g)
    tail = jnp.concatenate(tails, axis=1)
    tail_ref[0] = tail
    if carry:
        tail_s[j] = tail
    if final:
        @pl.when(j == n_steps - 1)
        def _():
            out_ref[...] = _rms(out_ref[...], fn_ref[...])


def _ffn_call(x, layer, ng, w_up, cw, cb, w_down, prev, fn, cast_next=(), *, tm, carry,
              tiles_per_seq):
    n_rows = x.shape[0]
    tf, tfs = (TF_FFN, TFS_FFN) if carry else (TF_FFN_SAMPLE, TF_FFN_SAMPLE)
    n_steps = D_FF // tf
    n_tiles = n_rows // tm
    tail_rows = SUBLANES if carry else tm
    final = fn is not None
    w_specs = [pl.BlockSpec((D_MODEL, tf), lambda i, j: (0, j)),
               pl.BlockSpec((D_MODEL, tf), lambda i, j: (0, j + n_steps)),
               pl.BlockSpec((tf, D_MODEL), lambda i, j: (j, 0))]
    in_specs = [
        pl.BlockSpec((tm, D_MODEL), lambda i, j: (i, 0)),
        pl.BlockSpec((None, 1, D_MODEL), lambda i, j: (layer, 0, 0)),
        w_specs[0],
        w_specs[1],
        pl.BlockSpec((None, 3, tf), lambda i, j: (layer, 0, j)),
        pl.BlockSpec((None, 1, tf), lambda i, j: (layer, 0, j)),
        w_specs[2],
    ]
    args = [x, ng, w_up, w_up, cw, cb, w_down]
    out_specs = [
        pl.BlockSpec((tm, D_MODEL), lambda i, j: (i, 0)),
        pl.BlockSpec((1, tail_rows, tf), lambda i, j: (i, 0, j)),
    ]
    out_shape = [
        jax.ShapeDtypeStruct((n_rows, D_MODEL), F32),
        jax.ShapeDtypeStruct((n_tiles, tail_rows, D_FF), F32),
    ]
    if not carry:
        n_groups, n_state = prev.shape[1:3]
        in_specs.append(pl.BlockSpec((None, n_groups, n_state, tf), lambda i, j: (layer, 0, 0, j)))
        args.append(prev)
    if final:
        in_specs.append(pl.BlockSpec((1, D_MODEL), lambda i, j: (0, 0)))
        args.append(fn)
    cast_in, cast_out, cast_shape = _cast_jobs(cast_next, layer + 1, n_tiles * n_steps,
                                               lambda i, j: i * n_steps + j)
    in_specs += cast_in
    args += list(cast_next)
    out_specs += cast_out
    out_shape += cast_shape
    scratch = [pltpu.VMEM((tm, D_MODEL), BF16)]
    if carry:
        scratch.append(pltpu.VMEM((n_steps, SUBLANES, tf), F32))
    return pl.pallas_call(
        functools.partial(_ffn_kernel, tm=tm, tf=tf, tfs=tfs, n_steps=n_steps,
                          tiles_per_seq=tiles_per_seq, carry=carry, final=final,
                          n_cast=len(cast_next)),
        grid=(n_tiles, n_steps),
        in_specs=in_specs,
        out_specs=out_specs,
        out_shape=out_shape,
        scratch_shapes=scratch,
        compiler_params=pltpu.CompilerParams(
            dimension_semantics=("arbitrary", "arbitrary"), vmem_limit_bytes=VMEM_LIMIT),
        name="conv_ffn_prompt" if carry else "conv_ffn_sample",
    )(*args)


def _rec_weight_specs(tc):
    nb = tc // RNN_BLOCK
    return [
        pl.BlockSpec((D_MODEL, tc), lambda i, j: (0, j)),
        pl.BlockSpec((D_MODEL, tc), lambda i, j: (0, j)),
        pl.BlockSpec((nb, RNN_BLOCK, RNN_BLOCK), lambda i, j: (j, 0, 0)),
        pl.BlockSpec((nb, RNN_BLOCK, RNN_BLOCK), lambda i, j: (j, 0, 0)),
        pl.BlockSpec((tc, D_MODEL), lambda i, j: (j, 0)),
    ]


def _rec_in_specs(layer, tm, tc, n_steps, stacked):
    nb = tc // RNN_BLOCK
    chunk = lambda i, j: (layer, 0, j)
    if stacked:
        w_gate, w_u, w_a, w_x, w_out = [
            pl.BlockSpec((None, D_MODEL, tc), chunk),
            pl.BlockSpec((None, D_MODEL, tc), lambda i, j: (layer, 0, j + n_steps)),
            pl.BlockSpec((None, nb, RNN_BLOCK, RNN_BLOCK), lambda i, j: (layer, j, 0, 0)),
            pl.BlockSpec((None, nb, RNN_BLOCK, RNN_BLOCK), lambda i, j: (layer, j, 0, 0)),
            pl.BlockSpec((None, tc, D_MODEL), lambda i, j: (layer, j, 0)),
        ]
    else:
        w_gate, w_u, w_a, w_x, w_out = _rec_weight_specs(tc)
    return [
        pl.BlockSpec((tm, D_MODEL), lambda i, j: (i, 0)),
        pl.BlockSpec((None, 1, D_MODEL), lambda i, j: (layer, 0, 0)),
        w_gate,
        w_u,
        pl.BlockSpec((None, 4, tc), chunk),
        pl.BlockSpec((None, 1, tc), chunk),
        w_a,
        pl.BlockSpec((None, 1, tc), chunk),
        w_x,
        pl.BlockSpec((None, 1, tc), chunk),
        pl.BlockSpec((None, 1, tc), chunk),
        w_out,
    ]


def _rec_prompt_kernel(x_ref, ng_ref, wgt_ref, wu_ref, cw_ref, cb_ref, wa_ref, ba_ref, wx_ref,
                       bx_ref, lam_ref, wo_ref, up_src, down_src, out_ref, utail_ref, hlast_ref,
                       up_dst, down_dst, h_s, utail_s, hc_s, *, tm, tc, tcs, n_steps,
                       tiles_per_seq):
    seg = tm // SUBLANES
    i = pl.program_id(0)
    j = pl.program_id(1)
    up_dst[...] = up_src[...].astype(BF16)
    down_dst[...] = down_src[...].astype(BF16)

    @pl.when(j == 0)
    def _():
        x = jnp.swapaxes(x_ref[...].reshape(SUBLANES, seg, D_MODEL), 0, 1)
        x = x.reshape(tm, D_MODEL)
        h_s[...] = _rms(x, ng_ref[...]).astype(BF16)
        out_ref[...] = x

    @pl.when((i % tiles_per_seq) == 0)
    def _():
        utail_s[j] = jnp.zeros((SUBLANES, tc), F32)
        hc_s[j] = jnp.zeros((SUBLANES, tc), F32)

    h = h_s[...]
    sub0 = lax.broadcasted_iota(jnp.int32, (SUBLANES, tcs), 0) == 0
    piece = lambda v, k: v[k * SUBLANES:(k + 1) * SUBLANES]
    tail_in = utail_s[j]
    hc_in = hc_s[j]
    sub_cols = [slice(sc * tcs, (sc + 1) * tcs) for sc in range(tc // tcs)]
    u_tails, h_lasts = [], []
    last = SUBLANES - 1
    us = [_dot(h, wu_ref[:, cols]) for cols in sub_cols]
    ucs_all = []
    for sc, cols in enumerate(sub_cols):
        up = [piece(us[sc], k) for k in range(seg)]
        tail = tail_in[:, cols]
        before = {}
        for d in (1, 2, 3):
            rolled = pltpu.roll(up[seg - d], 1, 0)
            before[-d] = jnp.where(sub0, tail[SUBLANES - d:SUBLANES - d + 1], rolled)
        at = lambda k: up[k] if k >= 0 else before[k]
        cw = cw_ref[:, cols]
        cb = cb_ref[:, cols]
        ucs = []
        for k in range(seg):
            v = cb + cw[0:1] * at(k - 3)
            v = v + cw[1:2] * at(k - 2)
            v = v + cw[2:3] * at(k - 1)
            ucs.append(v + cw[3:4] * up[k])
        ucs_all.append(jnp.concatenate(ucs, axis=0))
        u_tails.append(jnp.concatenate(
            [jnp.zeros((SUBLANES - 3, tcs), F32)] + [up[seg - d][last:last + 1] for d in (3, 2, 1)],
            axis=0))
    projected = []
    for sc, cols in enumerate(sub_cols):
        blocks = [sc * (tcs // RNN_BLOCK) + n for n in range(tcs // RNN_BLOCK)]
        ga, gx = _gate_dots(ucs_all[sc], wa_ref, wx_ref, blocks)
        projected.append((ga, gx, _dot(h, _after(wgt_ref[:, cols], ga))))
    acts = []
    for sc, cols in enumerate(sub_cols):
        ga, gx, gate = projected[sc]
        a, b = _lru_coeffs(ga, gx, ba_ref[:, cols], bx_ref[:, cols], lam_ref[:, cols],
                           ucs_all[sc])
        a_cum, b_cum = [piece(a, 0)], [piece(b, 0)]
        for k in range(1, seg):
            ak = piece(a, k)
            b_cum.append(ak * b_cum[-1] + piece(b, k))
            a_cum.append(ak * a_cum[-1])
        a_seg, b_seg = a_cum[-1], b_cum[-1]
        c = hc_in[0:1, cols]
        carries = [c]
        for s in range(SUBLANES):
            c = a_seg[s:s + 1] * c + b_seg[s:s + 1]
            carries.append(c)
        carry_in = jnp.concatenate(carries[:SUBLANES], axis=0)
        y = jnp.concatenate([a_cum[k] * carry_in + b_cum[k] for k in range(seg)], axis=0)
        acts.append((jax.nn.gelu(gate) * y).astype(BF16))
        h_lasts.append(jnp.broadcast_to(carries[SUBLANES], (SUBLANES, tcs)))
    for sc, cols in enumerate(sub_cols):
        out_ref[...] += _dot(acts[sc], wo_ref[cols, :])
    h_last = jnp.concatenate(h_lasts, axis=1)
    u_tail = jnp.concatenate(u_tails, axis=1)
    hc_s[j] = h_last
    hlast_ref[0] = h_last
    utail_s[j] = u_tail
    utail_ref[0] = u_tail

    @pl.when(j == n_steps - 1)
    def _():
        acc = jnp.swapaxes(out_ref[...].reshape(seg, SUBLANES, D_MODEL), 0, 1)
        out_ref[...] = acc.reshape(tm, D_MODEL)


def _rec_prompt_call(x, layer, prm, ffn_weights, *, tm, tiles_per_seq):
    n_rows = x.shape[0]
    tc, tcs = TC_RNN, TCS_RNN
    n_steps = D_RNN // tc
    n_tiles = n_rows // tm
    tail_spec = pl.BlockSpec((1, SUBLANES, tc), lambda i, j: (i, 0, j))
    tail_shape = jax.ShapeDtypeStruct((n_tiles, SUBLANES, D_RNN), F32)
    cast_in, cast_out, cast_shape = _cast_jobs(ffn_weights, layer, n_tiles * n_steps,
                                               lambda i, j: i * n_steps + j)
    return pl.pallas_call(
        functools.partial(_rec_prompt_kernel, tm=tm, tc=tc, tcs=tcs, n_steps=n_steps,
                          tiles_per_seq=tiles_per_seq),
        grid=(n_tiles, n_steps),
        in_specs=_rec_in_specs(layer, tm, tc, n_steps, stacked=False) + cast_in,
        out_specs=[pl.BlockSpec((tm, D_MODEL), lambda i, j: (i, 0)), tail_spec, tail_spec]
        + cast_out,
        out_shape=[jax.ShapeDtypeStruct((n_rows, D_MODEL), F32), tail_shape, tail_shape]
        + cast_shape,
        scratch_shapes=[pltpu.VMEM((tm, D_MODEL), BF16),
                        pltpu.VMEM((n_steps, SUBLANES, tc), F32),
                        pltpu.VMEM((n_steps, SUBLANES, tc), F32)],
        compiler_params=pltpu.CompilerParams(
            dimension_semantics=("arbitrary", "arbitrary"), vmem_limit_bytes=VMEM_LIMIT),
        name="recurrent_prompt",
    )(x, *prm, *ffn_weights)


def _rec_sample_kernel(x_ref, ng_ref, wgt_ref, wu_ref, cw_ref, cb_ref, wa_ref, ba_ref, wx_ref,
                       bx_ref, lam_ref, wo_ref, prev_ref, h0_ref, out_ref, utail_ref, ytail_ref,
                       wgt_o, wu_o, wa_o, wx_o, wo_o, h_s, *, tm, tc):
    j = pl.program_id(1)
    wgt, wu, wa, wx, wo = (w[...].astype(BF16) for w in (wgt_ref, wu_ref, wa_ref, wx_ref, wo_ref))
    for o_ref, w in zip((wgt_o, wu_o, wa_o, wx_o, wo_o), (wgt, wu, wa, wx, wo)):
        o_ref[...] = w

    @pl.when(j == 0)
    def _():
        x = x_ref[...]
        h_s[...] = _rms(x, ng_ref[...]).astype(BF16)
        out_ref[...] = x

    h = h_s[...]
    gate = _dot(h, wgt)
    u = _dot(h, wu)
    row8 = lax.broadcasted_iota(jnp.int32, (tm, tc), 0) & (GROUP_ROWS - 1)
    is_token = row8 >= STATE_ROWS
    u = jnp.where(is_token, u, _state_rows(prev_ref[...]))
    cw = cw_ref[...]
    uc = cb_ref[...] + cw[0:1] * _shifted(u, 3, 0)
    uc = uc + cw[1:2] * _shifted(u, 2, 0)
    uc = uc + cw[2:3] * _shifted(u, 1, 0)
    uc = uc + cw[3:4] * u
    ga, gx = _gate_dots(uc, wa, wx, list(range(tc // RNN_BLOCK)))
    a, b = _lru_coeffs(ga, gx, ba_ref[...], bx_ref[...], lam_ref[...], uc)
    a = jnp.where(is_token, a, 0.0)
    b = jnp.where(is_token, b, _state_rows(h0_ref[...]))
    for d in (1, 2, 4):
        m = row8 >= d
        a_sh = pltpu.roll(a, d, 0)
        b_sh = pltpu.roll(b, d, 0)
        b = jnp.where(m, a * b_sh + b, b)
        a = jnp.where(m, a * a_sh, a)
    act = (jax.nn.gelu(gate) * b).astype(BF16)
    out_ref[...] += _dot(act, wo)
    utail_ref[...] = u
    ytail_ref[...] = b


def _rec_sample_call(x, layer, prm, prev, h0, *, tm):
    n_rows = x.shape[0]
    tc = TC_RNN_SAMPLE
    n_steps = D_RNN // tc
    state_spec = pl.BlockSpec((tm, tc), lambda i, j: (i, j))
    state_shape = jax.ShapeDtypeStruct((n_rows, D_RNN), F32)
    return pl.pallas_call(
        functools.partial(_rec_sample_kernel, tm=tm, tc=tc),
        grid=(n_rows // tm, n_steps),
        in_specs=_rec_in_specs(layer, tm, tc, n_steps, stacked=True) + [
            pl.BlockSpec((None,) + prev.shape[1:3] + (tc,), lambda i, j: (layer, 0, 0, j)),
            pl.BlockSpec((None,) + h0.shape[1:3] + (tc,), lambda i, j: (layer, 0, 0, j))],
        out_specs=[pl.BlockSpec((tm, D_MODEL), lambda i, j: (i, 0)), state_spec, state_spec]
        + _rec_weight_specs(tc),
        out_shape=[jax.ShapeDtypeStruct((n_rows, D_MODEL), F32), state_shape, state_shape,
                   jax.ShapeDtypeStruct((D_MODEL, D_RNN), BF16),
                   jax.ShapeDtypeStruct((D_MODEL, D_RNN), BF16),
                   jax.ShapeDtypeStruct((D_RNN // RNN_BLOCK, RNN_BLOCK, RNN_BLOCK), BF16),
                   jax.ShapeDtypeStruct((D_RNN // RNN_BLOCK, RNN_BLOCK, RNN_BLOCK), BF16),
                   jax.ShapeDtypeStruct((D_RNN, D_MODEL), BF16)],
        scratch_shapes=[pltpu.VMEM((tm, D_MODEL), BF16)],
        compiler_params=pltpu.CompilerParams(
            dimension_semantics=("arbitrary", "arbitrary"), vmem_limit_bytes=VMEM_LIMIT),
        name="recurrent_sample",
    )(x, *prm, prev, h0)


def _kv_kernel(x_ref, ng_ref, wk_ref, wv_ref, cos_ref, slo_ref, shi_ref, k_ref, v_ref):
    h = _rms(x_ref[...], ng_ref[...]).astype(BF16)
    k = _dot(h, wk_ref[...].astype(BF16))
    k_ref[...] = _rope(k, cos_ref[...], slo_ref[...], shi_ref[...])
    v_ref[...] = _dot(h, wv_ref[...].astype(BF16))


def _kv_call(x, ng, w_k, w_v, tables, *, tm):
    n_rows = x.shape[0]
    table_tiles = tables[0].shape[0] // tm
    row = lambda i: (i, 0)
    pos = lambda i: (i % table_tiles, 0)
    fixed = lambda i: (0, 0)
    return pl.pallas_call(
        _kv_kernel,
        grid=(n_rows // tm,),
        in_specs=[
            pl.BlockSpec((tm, D_MODEL), row),
            pl.BlockSpec((1, D_MODEL), fixed),
            pl.BlockSpec((D_MODEL, KV_LANES), fixed),
            pl.BlockSpec((D_MODEL, KV_LANES), fixed),
            pl.BlockSpec((tm, LANES), pos),
            pl.BlockSpec((tm, LANES), pos),
            pl.BlockSpec((tm, LANES), pos),
        ],
        out_specs=[pl.BlockSpec((tm, KV_LANES), row), pl.BlockSpec((tm, KV_LANES), row)],
        out_shape=[jax.ShapeDtypeStruct((n_rows, KV_LANES), F32)] * 2,
        compiler_params=pltpu.CompilerParams(
            dimension_semantics=("arbitrary",), vmem_limit_bytes=VMEM_LIMIT),
        name="shared_kv",
    )(x, ng, w_k, w_v, *tables)


def _scores(qs, kp, first_pair):
    rows = qs[0].shape[0]
    n = len(qs)
    hi_half = lax.broadcasted_iota(jnp.int32, (rows, LANES), 1) >= HEAD_DIM
    zeros = jnp.zeros((rows, LANES), F32)
    blocks, heads, keeps = [], [], []
    for t, q in enumerate(qs):
        place = lambda v: jnp.concatenate([zeros] * t + [v] + [zeros] * (n - 1 - t), axis=1)
        for gl in range(4):
            p = gl // 2
            keep = hi_half if p else jnp.logical_not(hi_half)
            qg = q[:, gl * LANES:(gl + 1) * LANES]
            blocks.append(place(jnp.where(keep, qg, 0.0)))
            blocks.append(place(jnp.where(keep, pltpu.roll(qg, HEAD_DIM, 1), 0.0)))
            g = 4 * (first_pair + t) + gl
            heads += [2 * g + p, 2 * g + 1 - p]
            keeps.append(keep)
    stacked = (jnp.concatenate(blocks, axis=0) * (HEAD_DIM ** -0.5)).astype(BF16)
    s = lax.dot_general(stacked, kp, (((1,), (1,)), ((), ())), preferred_element_type=F32)
    return s, heads, keeps


def _attend(scored, vp, allowed, sink_ref, layer):
    s, heads, keeps = scored
    n_blocks = len(heads)
    rows = s.shape[0] // n_blocks
    probs = []
    for bi in range(n_blocks):
        sb = jnp.where(allowed, s[bi * rows:(bi + 1) * rows], -jnp.inf)
        sink = sink_ref[layer, heads[bi]]
        m = jnp.maximum(jnp.max(sb, axis=-1, keepdims=True), sink)
        pe = jnp.exp(sb - m)
        den = jnp.sum(pe, axis=-1, keepdims=True) + jnp.exp(sink - m)
        probs.append(pe / den)
    ps = jnp.concatenate(probs, axis=0).astype(BF16)
    o = _dot(ps, vp)
    outs = []
    for gi in range(n_blocks // 2):
        lanes = slice((gi // 4) * LANES, (gi // 4 + 1) * LANES)
        o0 = o[(2 * gi) * rows:(2 * gi + 1) * rows, lanes]
        o1 = o[(2 * gi + 1) * rows:(2 * gi + 2) * rows, lanes]
        outs.append(jnp.where(keeps[gi], o0, pltpu.roll(o1, HEAD_DIM, 1)))
    return jnp.concatenate(outs, axis=1)


def _attn_kernel(*refs, layer, tm, tiles_per_seq, prompt, pps):
    it = iter(refs)
    sink_ref, x_ref, ng_ref, wq_ref, cos_ref, slo_ref, shi_ref, wo_ref = (next(it) for _ in range(8))
    ka_ref, va_ref, kb_ref, vb_ref = (next(it) for _ in range(4))
    out_ref = next(it)
    wq_o, wo_o = (None, None) if prompt else (next(it), next(it))
    h_s = next(it)
    i = pl.program_id(0)
    j = pl.program_id(1)
    if prompt:
        wq, wo = wq_ref, wo_ref[...]
    else:
        wq, wo = wq_ref[...].astype(BF16), wo_ref[...].astype(BF16)
        wq_o[...] = wq
        wo_o[...] = wo

    @pl.when(j == 0)
    def _():
        x = x_ref[...]
        h_s[...] = _rms(x, ng_ref[...]).astype(BF16)
        out_ref[...] = x

    h = h_s[...]
    rope = lambda v: _rope(v, cos_ref[...], slo_ref[...], shi_ref[...])
    blk = ATTN_BLOCK
    if prompt:
        first = (i % tiles_per_seq) == 0
        r = lax.broadcasted_iota(jnp.int32, (blk, 2 * blk), 0)
        kk = lax.broadcasted_iota(jnp.int32, (blk, 2 * blk), 1)
        band = (kk > r) & (kk <= r + blk)
        band0 = band & (kk >= jnp.where(first, blk, 0))
        qs = [rope(_dot(h, wq[:, p * PAIR_W:(p + 1) * PAIR_W])) for p in range(pps)]
        scored, values, masks = [], [], []
        for qb in range(tm // blk):
            cur = slice(qb * blk, (qb + 1) * blk)
            if qb == 0:
                k_prev, v_prev = kb_ref[...], vb_ref[...]
            else:
                prv = slice((qb - 1) * blk, qb * blk)
                k_prev, v_prev = ka_ref[prv, :], va_ref[prv, :]
            masks.append(band0 if qb == 0 else band)
            kp = jnp.concatenate([k_prev, ka_ref[cur, :]], axis=0).astype(BF16)
            values.append(jnp.concatenate([v_prev, va_ref[cur, :]], axis=0).astype(BF16))
            scored.append(_scores([q[cur] for q in qs], kp, j * pps))
        o = jnp.concatenate(
            [_attend(sc, vp, allowed, sink_ref, layer)
             for sc, vp, allowed in zip(scored, values, masks)], axis=0)
    else:
        q = rope(_dot(h, wq))
        n_groups = tm // GROUP_ROWS
        t = lax.broadcasted_iota(jnp.int32, (GROUP_ROWS, 2 * blk), 0) - STATE_ROWS
        kk = lax.broadcasted_iota(jnp.int32, (GROUP_ROWS, 2 * blk), 1)
        new_t = kk - blk - STATE_ROWS
        allowed = ((kk < blk) & (kk > t)) | ((kk >= blk) & (new_t >= 0) & (new_t <= t) &
                                               (kk < blk + GROUP_ROWS))
        allowed = allowed & (t >= 0)
        pad = jnp.zeros((blk - GROUP_ROWS, LANES), F32)
        scored, values = [], []
        for b in range(n_groups):
            rows = slice(b * GROUP_ROWS, (b + 1) * GROUP_ROWS)
            kp = jnp.concatenate([ka_ref[b], kb_ref[rows, :], pad], axis=0).astype(BF16)
            values.append(jnp.concatenate([va_ref[b], vb_ref[rows, :], pad], axis=0).astype(BF16))
            scored.append(_scores([q[rows]], kp, j))
        o = jnp.concatenate(
            [_attend(sc, vp, allowed, sink_ref, layer) for sc, vp in zip(scored, values)], axis=0)
    out_ref[...] += _dot(o.astype(BF16), wo)


def _attn_call(x, layer, ng, w_q, w_o, sinks, tables, ka, va, kb, vb, *, tm, tiles_per_seq, prompt):
    n_rows = x.shape[0]
    n_tiles = n_rows // tm
    pps = PAIRS_PER_STEP if prompt else 1
    row = lambda i, j: (i, 0)
    pos = lambda i, j: (i % tiles_per_seq, 0)
    if prompt:
        per = tm // ATTN_BLOCK
        cur_spec = pl.BlockSpec((tm, pps * LANES), lambda i, j: (i, j))
        prev_spec = pl.BlockSpec((ATTN_BLOCK, pps * LANES),
                                 lambda i, j: (jnp.maximum(i * per - 1, 0), j))
        kv_specs = [cur_spec, cur_spec, prev_spec, prev_spec]
    else:
        n_groups = tm // GROUP_ROWS
        cache_spec = pl.BlockSpec((n_groups, ATTN_BLOCK, LANES), lambda i, j: (0, 0, j))
        new_spec = pl.BlockSpec((tm, LANES), lambda i, j: (i, j))
        kv_specs = [cache_spec, cache_spec, new_spec, new_spec]
    wq_copy = pl.BlockSpec((D_MODEL, pps * PAIR_W), lambda i, j: (0, j))
    wo_copy = pl.BlockSpec((pps * PAIR_W, D_MODEL), lambda i, j: (j, 0))
    x_shape = jax.ShapeDtypeStruct((n_rows, D_MODEL), F32)
    if prompt:
        wq_spec, wo_spec = wq_copy, wo_copy
        out_specs, out_shape = pl.BlockSpec((tm, D_MODEL), row), x_shape
    else:
        wq_spec = pl.BlockSpec((None, D_MODEL, PAIR_W), lambda i, j: (layer, 0, j))
        wo_spec = pl.BlockSpec((None, PAIR_W, D_MODEL), lambda i, j: (layer, j, 0))
        out_specs = [pl.BlockSpec((tm, D_MODEL), row), wq_copy, wo_copy]
        out_shape = [x_shape, jax.ShapeDtypeStruct((D_MODEL, N_HEADS * HEAD_DIM), BF16),
                     jax.ShapeDtypeStruct((N_HEADS * HEAD_DIM, D_MODEL), BF16)]
    return pl.pallas_call(
        functools.partial(_attn_kernel, layer=layer, tm=tm, tiles_per_seq=tiles_per_seq,
                          prompt=prompt, pps=pps),
        grid=(n_tiles, N_PAIRS // pps),
        in_specs=[
            pl.BlockSpec(memory_space=pltpu.SMEM),
            pl.BlockSpec((tm, D_MODEL), row),
            pl.BlockSpec((None, 1, D_MODEL), lambda i, j: (layer, 0, 0)),
            wq_spec,
            pl.BlockSpec((tm, LANES), pos),
            pl.BlockSpec((tm, LANES), pos),
            pl.BlockSpec((tm, LANES), pos),
            wo_spec,
        ] + kv_specs,
        out_specs=out_specs,
        out_shape=out_shape,
        scratch_shapes=[pltpu.VMEM((tm, D_MODEL), BF16)],
        compiler_params=pltpu.CompilerParams(
            dimension_semantics=("arbitrary", "arbitrary"), vmem_limit_bytes=VMEM_LIMIT),
        name="attention_prompt" if prompt else "attention_sample",
    )(sinks, x, ng, w_q, *tables, w_o, ka, va, kb, vb)


def _rope_tables(pos):
    half = HEAD_DIM // 2
    freqs = ROPE_THETA ** (-jnp.arange(half, dtype=F32) / half)
    ang = pos.astype(F32)[:, None] * freqs[None, :]
    cos, sin = jnp.cos(ang), jnp.sin(ang)
    zero = jnp.zeros_like(sin)
    cos_t = jnp.concatenate([cos, cos] * 2, axis=1)
    sin_lo = jnp.concatenate([-sin, zero] * 2, axis=1)
    sin_hi = jnp.concatenate([zero, sin] * 2, axis=1)
    return cos_t, sin_lo, sin_hi


def kernel(x_prompt, x_sample, state_rglru_h, state_rglru_conv, state_ffn_conv, cache_window_k,
           cache_window_v, a_norm, a_w_in, a_conv_w, a_conv_b, a_gate_a_w, a_gate_a_b, a_gate_x_w,
           a_gate_x_b, a_lambda, a_w_out, kv_norm, w_k, w_v, b_norm, w_q, sinks, w_o, f_norm,
           f_w_up, f_conv_w, f_conv_b, f_w_down, final_norm):
    bp, s_len, _ = x_prompt.shape
    bs, t_len, _ = x_sample.shape
    depth = f_norm.shape[0]
    n_a = a_norm.shape[0]
    assert t_len == GROUP_ROWS - STATE_ROWS and cache_window_k.shape[1] == ATTN_BLOCK
    tiles_per_seq = s_len // TM_PROMPT
    ts = bs * GROUP_ROWS

    xp = x_prompt.reshape(bp * s_len, D_MODEL)
    xs = jnp.pad(x_sample, ((0, 0), (STATE_ROWS, 0), (0, 0))).reshape(ts, D_MODEL)
    pos_p = jnp.arange(s_len, dtype=jnp.int32)
    pos_s = jnp.tile(PAST_LEN - STATE_ROWS + jnp.arange(GROUP_ROWS, dtype=jnp.int32), bs)
    tab_p = _rope_tables(pos_p)
    tab_s = _rope_tables(pos_s)

    vec = lambda v: v.reshape(v.shape[:-1] + (1, v.shape[-1]))
    a_norm3, a_conv_b3, a_lambda3 = vec(a_norm), vec(a_conv_b), vec(a_lambda)
    a_gate_a_b3, a_gate_x_b3 = vec(a_gate_a_b), vec(a_gate_x_b)
    f_norm3, f_conv_b3, b_norm3 = vec(f_norm), vec(f_conv_b), vec(b_norm)

    def rec_params(w_gate_in, w_u_in, w_a, w_x, w_out):
        return (a_norm3, w_gate_in, w_u_in, a_conv_w, a_conv_b3, w_a, a_gate_a_b3, w_x,
                a_gate_x_b3, a_lambda3, w_out)
    cache_k = cache_window_k.reshape(bs, ATTN_BLOCK, KV_LANES)
    cache_v = cache_window_v.reshape(bs, ATTN_BLOCK, KV_LANES)
    rec_h0 = state_rglru_h[:, :, None, :]
    last = slice(tiles_per_seq - 1, None, tiles_per_seq)

    h_p, h_s, c_p, c_s, f_p, f_s = [], [], [], [], [], []
    for layer in range(depth):
        if layer < n_a:
            prm = rec_params(a_w_in, a_w_in, a_gate_a_w, a_gate_x_w, a_w_out)
            xs, ut, yt, *w_b = _rec_sample_call(xs, layer, prm, state_rglru_conv, rec_h0,
                                                tm=ts)
            c_s.append(ut.reshape(bs, GROUP_ROWS, D_RNN)[:, GROUP_ROWS - 3:])
            h_s.append(yt.reshape(bs, GROUP_ROWS, D_RNN)[:, GROUP_ROWS - 1])
            xp, ut, ht, w_up_b, w_down_b = _rec_prompt_call(
                xp, layer, rec_params(*w_b), (f_w_up, f_w_down), tm=TM_PROMPT,
                tiles_per_seq=tiles_per_seq)
            c_p.append(ut[last, SUBLANES - 3:])
            h_p.append(ht[last, 0])
        else:
            j = layer - n_a
            if j == 0:
                k_s, v_s = _kv_call(xs, vec(kv_norm), w_k, w_v, tab_s, tm=ts)
                k_p, v_p = _kv_call(xp, vec(kv_norm), w_k, w_v, tab_p, tm=TM_PROMPT)
            xs, w_q_b, w_o_b = _attn_call(xs, j, b_norm3, w_q, w_o, sinks, tab_s, cache_k, cache_v,
                                          k_s, v_s, tm=ts, tiles_per_seq=1, prompt=False)
            xp = _attn_call(xp, j, b_norm3, w_q_b, w_o_b, sinks, tab_p, k_p, v_p, k_p, v_p,
                            tm=TM_PROMPT, tiles_per_seq=tiles_per_seq, prompt=True)
        fn = vec(final_norm) if layer == depth - 1 else None
        xs, gt = _ffn_call(xs, layer, f_norm3, w_up_b, f_conv_w, f_conv_b3, w_down_b,
                           state_ffn_conv, fn, tm=ts, carry=False, tiles_per_seq=1)
        f_s.append(gt.reshape(bs, GROUP_ROWS, D_FF)[:, GROUP_ROWS - 2:])
        cast_next = (f_w_up, f_w_down) if n_a <= layer + 1 < depth else ()
        xp, gt, *w_next = _ffn_call(xp, layer, f_norm3, w_up_b, f_conv_w, f_conv_b3, w_down_b, None,
                                    fn, cast_next, tm=TM_PROMPT, carry=True,
                                    tiles_per_seq=tiles_per_seq)
        f_p.append(gt[last, SUBLANES - 2:])
        if w_next:
            w_up_b, w_down_b = w_next

    y_prompt = xp.reshape(bp, s_len, D_MODEL)
    y_sample = xs.reshape(bs, GROUP_ROWS, D_MODEL)[:, STATE_ROWS:]
    heads = lambda a: a.reshape(a.shape[0], a.shape[1], N_KV_HEADS, HEAD_DIM)
    win_p = min(ATTN_BLOCK, s_len)
    k_p3 = k_p.reshape(bp, s_len, KV_LANES)
    v_p3 = v_p.reshape(bp, s_len, KV_LANES)
    k_new = k_s.reshape(bs, GROUP_ROWS, KV_LANES)[:, STATE_ROWS:]
    v_new = v_s.reshape(bs, GROUP_ROWS, KV_LANES)[:, STATE_ROWS:]
    win_k = jnp.concatenate([cache_k[:, t_len:], k_new], axis=1)
    win_v = jnp.concatenate([cache_v[:, t_len:], v_new], axis=1)
    return (y_prompt, y_sample, jnp.stack(h_p), jnp.stack(h_s), jnp.stack(c_p), jnp.stack(c_s),
            jnp.stack(f_p), jnp.stack(f_s), heads(k_p3[:, s_len - win_p:]),
            heads(v_p3[:, s_len - win_p:]), heads(win_k), heads(win_v))
```

```python
import functools

import jax
import jax.numpy as jnp
from jax import lax
from jax.experimental import pallas as pl
from jax.experimental.pallas import tpu as pltpu

F32 = jnp.float32
BF16 = jnp.bfloat16

D_MODEL = 2048
PAST_LEN = 16384
D_RNN = 2048
RNN_BLOCK = 256
LRU_C = 8.0
N_HEADS = 32
HEAD_DIM = 64
N_KV_HEADS = 8
ATTN_BLOCK = 128
ROPE_THETA = 10000.0
D_FF = 3 * D_MODEL
EPS = 1e-6

LANES = 128
SUBLANES = 8
GROUP_ROWS = SUBLANES
STATE_ROWS = 4
KV_LANES = N_KV_HEADS * HEAD_DIM
PAIR_W = 4 * LANES
N_PAIRS = N_KV_HEADS // 2
PAIRS_PER_STEP = 2
VMEM_LIMIT = 56 * 1024 * 1024

TM_PROMPT = 512
SEG = TM_PROMPT // SUBLANES
TC_RNN = 512
TCS_RNN = 256
TC_RNN_SAMPLE = 256
TF_FFN = 1024
TFS_FFN = 512
TF_FFN_SAMPLE = 512


def _rms(x, g):
    ms = jnp.mean(x * x, axis=-1, keepdims=True)
    return x * lax.rsqrt(ms + EPS) * g


def _dot(a, b):
    return jnp.dot(a, b, preferred_element_type=F32)


def _shifted(xe, k, drop):
    return pltpu.roll(xe, k, 0)[drop:]


def _rope(x, cos, sin_lo, sin_hi):
    w = x.shape[1]
    n = w // LANES
    cosw = jnp.concatenate([cos] * n, axis=1)
    lo = jnp.concatenate([sin_lo] * n, axis=1)
    hi = jnp.concatenate([sin_hi] * n, axis=1)
    half = HEAD_DIM // 2
    return x * cosw + pltpu.roll(x, w - half, 1) * lo + pltpu.roll(x, half, 1) * hi


def _state_rows(state):
    g, n, c = state.shape
    padded = jnp.pad(state, ((0, 0), (STATE_ROWS - n, GROUP_ROWS - STATE_ROWS), (0, 0)))
    return padded.reshape(g * GROUP_ROWS, c)


def _after(w, dep):
    rows = 2 * SUBLANES
    zero = jnp.minimum(jnp.abs(dep[0:rows, 0:LANES]), 0.0).astype(BF16)
    parts = []
    for r0 in range(0, w.shape[0], 256):
        head = w[r0:r0 + rows, 0:LANES] + zero
        if w.shape[1] > LANES:
            head = jnp.concatenate([head, w[r0:r0 + rows, LANES:]], axis=1)
        parts += [head, w[r0 + rows:min(r0 + 256, w.shape[0])]]
    return jnp.concatenate(parts, axis=0)


def _cast_jobs(srcs, layer, total_steps, step):
    in_specs, out_specs, out_shape = [], [], []
    for src in srcs:
        _, r, c = src.shape
        if r % (total_steps * 2 * SUBLANES) == 0:
            rps = r // total_steps
            in_specs.append(pl.BlockSpec((None, rps, c), lambda i, j: (layer, step(i, j), 0)))
            out_specs.append(pl.BlockSpec((rps, c), lambda i, j: (step(i, j), 0)))
        else:
            cps = c // total_steps
            assert cps * total_steps == c and cps % LANES == 0
            in_specs.append(pl.BlockSpec((None, r, cps), lambda i, j: (layer, 0, step(i, j))))
            out_specs.append(pl.BlockSpec((r, cps), lambda i, j: (0, step(i, j))))
        out_shape.append(jax.ShapeDtypeStruct((r, c), BF16))
    return in_specs, out_specs, out_shape


def _gate_dots(uc, wa_ref, wx_ref, blocks):
    ucb = uc.astype(BF16)
    both = [_dot(ucb[:, n * RNN_BLOCK:(n + 1) * RNN_BLOCK],
                 jnp.concatenate([wa_ref[bk], wx_ref[bk]], axis=1)) for n, bk in enumerate(blocks)]
    ga = jnp.concatenate([g[:, :RNN_BLOCK] for g in both], axis=1)
    gx = jnp.concatenate([g[:, RNN_BLOCK:] for g in both], axis=1)
    return ga, gx


def _lru_coeffs(ga, gx, ba, bx, lam, uc):
    r = jax.nn.sigmoid(ga + ba)
    ig = jax.nn.sigmoid(gx + bx)
    z = -lam
    softplus = jnp.maximum(z, 0.0) + jnp.log1p(jnp.exp(-jnp.abs(z)))
    log_a = (-LRU_C * r) * softplus
    a = jnp.exp(log_a)
    sq = -jnp.tanh(log_a) * (a * a + 1.0)
    mult = jnp.where(sq > 0.0, sq * lax.rsqrt(sq), 0.0)
    b = mult * ig * uc
    return a, b


def _ffn_kernel(*refs, tm, tf, tfs, n_steps, tiles_per_seq, carry, final, n_cast):
    it = iter(refs)
    x_ref, ng_ref, wg, wu, cw_ref, cb_ref, wd = (next(it) for _ in range(7))
    prev_ref = None if carry else next(it)
    fn_ref = next(it) if final else None
    cast_src = [next(it) for _ in range(n_cast)]
    out_ref, tail_ref = next(it), next(it)
    cast_dst = [next(it) for _ in range(n_cast)]
    h_s = next(it)
    tail_s = next(it) if carry else None
    i = pl.program_id(0)
    j = pl.program_id(1)

    @pl.when(j == 0)
    def _():
        x = x_ref[...]
        h_s[...] = _rms(x, ng_ref[...]).astype(BF16)
        out_ref[...] = x

    if carry:
        @pl.when((i % tiles_per_seq) == 0)
        def _():
            tail_s[j] = jnp.zeros((SUBLANES, tf), F32)
    else:
        row = lax.broadcasted_iota(jnp.int32, (tm, tfs), 0)
        is_token = (row & (GROUP_ROWS - 1)) >= STATE_ROWS
    for src, dst in zip(cast_src, cast_dst):
        dst[...] = src[...].astype(BF16)

    h = h_s[...]
    tail_in = tail_s[j] if carry else None
    tails = []
    for sc in range(tf // tfs):
        cols = slice(sc * tfs, (sc + 1) * tfs)
        g = _dot(h, wg[:, cols])
        u = _dot(h, wu[:, cols])
        if carry:
            ge = jnp.concatenate([tail_in[:, cols], g], axis=0)
            drop = SUBLANES
        else:
            g = jnp.where(is_token, g, _state_rows(prev_ref[:, :, cols]))
            ge = g
            drop = 0
        gc = cb_ref[:, cols] + cw_ref[0:1, cols] * _shifted(ge, 2, drop)
        gc = gc + cw_ref[1:2, cols] * _shifted(ge, 1, drop)
        gc = gc + cw_ref[2:3, cols] * g
        act = (jax.nn.gelu(gc) * u).astype(BF16)
        out_ref[...] += _dot(act, wd[cols, :])
        tails.append(g[tm - SUBLANES:] if carry else g)
    tail = jnp.concatenate(tails, axis=1)
    tail_ref[0] = tail
    if carry:
        tail_s[j] = tail
    if final:
        @pl.when(j == n_steps - 1)
        def _():
            out_ref[...] = _rms(out_ref[...], fn_ref[...])


def _ffn_call(x, layer, ng, w_up, cw, cb, w_down, prev, fn, cast_next=(), *, tm, carry,
              tiles_per_seq):
    n_rows = x.shape[0]
    tf, tfs = (TF_FFN, TFS_FFN) if carry else (TF_FFN_SAMPLE, TF_FFN_SAMPLE)
    n_steps = D_FF // tf
    n_tiles = n_rows // tm
    tail_rows = SUBLANES if carry else tm
    final = fn is not None
    w_specs = [pl.BlockSpec((D_MODEL, tf), lambda i, j: (0, j)),
               pl.BlockSpec((D_MODEL, tf), lambda i, j: (0, j + n_steps)),
               pl.BlockSpec((tf, D_MODEL), lambda i, j: (j, 0))]
    in_specs = [
        pl.BlockSpec((tm, D_MODEL), lambda i, j: (i, 0)),
        pl.BlockSpec((None, 1, D_MODEL), lambda i, j: (layer, 0, 0)),
        w_specs[0],
        w_specs[1],
        pl.BlockSpec((None, 3, tf), lambda i, j: (layer, 0, j)),
        pl.BlockSpec((None, 1, tf), lambda i, j: (layer, 0, j)),
        w_specs[2],
    ]
    args = [x, ng, w_up, w_up, cw, cb, w_down]
    out_specs = [
        pl.BlockSpec((tm, D_MODEL), lambda i, j: (i, 0)),
        pl.BlockSpec((1, tail_rows, tf), lambda i, j: (i, 0, j)),
    ]
    out_shape = [
        jax.ShapeDtypeStruct((n_rows, D_MODEL), F32),
        jax.ShapeDtypeStruct((n_tiles, tail_rows, D_FF), F32),
    ]
    if not carry:
        n_groups, n_state = prev.shape[1:3]
        in_specs.append(pl.BlockSpec((None, n_groups, n_state, tf), lambda i, j: (layer, 0, 0, j)))
        args.append(prev)
    if final:
        in_specs.append(pl.BlockSpec((1, D_MODEL), lambda i, j: (0, 0)))
        args.append(fn)
    cast_in, cast_out, cast_shape = _cast_jobs(cast_next, layer + 1, n_tiles * n_steps,
                                               lambda i, j: i * n_steps + j)
    in_specs += cast_in
    args += list(cast_next)
    out_specs += cast_out
    out_shape += cast_shape
    scratch = [pltpu.VMEM((tm, D_MODEL), BF16)]
    if carry:
        scratch.append(pltpu.VMEM((n_steps, SUBLANES, tf), F32))
    return pl.pallas_call(
        functools.partial(_ffn_kernel, tm=tm, tf=tf, tfs=tfs, n_steps=n_steps,
                          tiles_per_seq=tiles_per_seq, carry=carry, final=final,
                          n_cast=len(cast_next)),
        grid=(n_tiles, n_steps),
        in_specs=in_specs,
        out_specs=out_specs,
        out_shape=out_shape,
        scratch_shapes=scratch,
        compiler_params=pltpu.CompilerParams(
            dimension_semantics=("arbitrary", "arbitrary"), vmem_limit_bytes=VMEM_LIMIT),
        name="conv_ffn_prompt" if carry else "conv_ffn_sample",
    )(*args)


def _rec_weight_specs(tc):
    nb = tc // RNN_BLOCK
    return [
        pl.BlockSpec((D_MODEL, tc), lambda i, j: (0, j)),
        pl.BlockSpec((D_MODEL, tc), lambda i, j: (0, j)),
        pl.BlockSpec((nb, RNN_BLOCK, RNN_BLOCK), lambda i, j: (j, 0, 0)),
        pl.BlockSpec((nb, RNN_BLOCK, RNN_BLOCK), lambda i, j: (j, 0, 0)),
        pl.BlockSpec((tc, D_MODEL), lambda i, j: (j, 0)),
    ]


def _rec_in_specs(layer, tm, tc, n_steps, stacked):
    nb = tc // RNN_BLOCK
    chunk = lambda i, j: (layer, 0, j)
    if stacked:
        w_gate, w_u, w_a, w_x, w_out = [
            pl.BlockSpec((None, D_MODEL, tc), chunk),
            pl.BlockSpec((None, D_MODEL, tc), lambda i, j: (layer, 0, j + n_steps)),
            pl.BlockSpec((None, nb, RNN_BLOCK, RNN_BLOCK), lambda i, j: (layer, j, 0, 0)),
            pl.BlockSpec((None, nb, RNN_BLOCK, RNN_BLOCK), lambda i, j: (layer, j, 0, 0)),
            pl.BlockSpec((None, tc, D_MODEL), lambda i, j: (layer, j, 0)),
        ]
    else:
        w_gate, w_u, w_a, w_x, w_out = _rec_weight_specs(tc)
    return [
        pl.BlockSpec((tm, D_MODEL), lambda i, j: (i, 0)),
        pl.BlockSpec((None, 1, D_MODEL), lambda i, j: (layer, 0, 0)),
        w_gate,
        w_u,
        pl.BlockSpec((None, 4, tc), chunk),
        pl.BlockSpec((None, 1, tc), chunk),
        w_a,
        pl.BlockSpec((None, 1, tc), chunk),
        w_x,
        pl.BlockSpec((None, 1, tc), chunk),
        pl.BlockSpec((None, 1, tc), chunk),
        w_out,
    ]


def _rec_prompt_kernel(x_ref, ng_ref, wgt_ref, wu_ref, cw_ref, cb_ref, wa_ref, ba_ref, wx_ref,
                       bx_ref, lam_ref, wo_ref, up_src, down_src, out_ref, utail_ref, hlast_ref,
                       up_dst, down_dst, h_s, utail_s, hc_s, *, tm, tc, tcs, n_steps,
                       tiles_per_seq):
    seg = tm // SUBLANES
    i = pl.program_id(0)
    j = pl.program_id(1)
    up_dst[...] = up_src[...].astype(BF16)
    down_dst[...] = down_src[...].astype(BF16)

    @pl.when(j == 0)
    def _():
        x = jnp.swapaxes(x_ref[...].reshape(SUBLANES, seg, D_MODEL), 0, 1)
        x = x.reshape(tm, D_MODEL)
        h_s[...] = _rms(x, ng_ref[...]).astype(BF16)
        out_ref[...] = x

    @pl.when((i % tiles_per_seq) == 0)
    def _():
        utail_s[j] = jnp.zeros((SUBLANES, tc), F32)
        hc_s[j] = jnp.zeros((SUBLANES, tc), F32)

    h = h_s[...]
    sub0 = lax.broadcasted_iota(jnp.int32, (SUBLANES, tcs), 0) == 0
    piece = lambda v, k: v[k * SUBLANES:(k + 1) * SUBLANES]
    tail_in = utail_s[j]
    hc_in = hc_s[j]
    sub_cols = [slice(sc * tcs, (sc + 1) * tcs) for sc in range(tc // tcs)]
    u_tails, h_lasts = [], []
    last = SUBLANES - 1
    us = [_dot(h, wu_ref[:, cols]) for cols in sub_cols]
    ucs_all = []
    for sc, cols in enumerate(sub_cols):
        up = [piece(us[sc], k) for k in range(seg)]
        tail = tail_in[:, cols]
        before = {}
        for d in (1, 2, 3):
            rolled = pltpu.roll(up[seg - d], 1, 0)
            before[-d] = jnp.where(sub0, tail[SUBLANES - d:SUBLANES - d + 1], rolled)
        at = lambda k: up[k] if k >= 0 else before[k]
        cw = cw_ref[:, cols]
        cb = cb_ref[:, cols]
        ucs = []
        for k in range(seg):
            v = cb + cw[0:1] * at(k - 3)
            v = v + cw[1:2] * at(k - 2)
            v = v + cw[2:3] * at(k - 1)
            ucs.append(v + cw[3:4] * up[k])
        ucs_all.append(jnp.concatenate(ucs, axis=0))
        u_tails.append(jnp.concatenate(
            [jnp.zeros((SUBLANES - 3, tcs), F32)] + [up[seg - d][last:last + 1] for d in (3, 2, 1)],
            axis=0))
    projected = []
    for sc, cols in enumerate(sub_cols):
        blocks = [sc * (tcs // RNN_BLOCK) + n for n in range(tcs // RNN_BLOCK)]
        ga, gx = _gate_dots(ucs_all[sc], wa_ref, wx_ref, blocks)
        projected.append((ga, gx, _dot(h, _after(wgt_ref[:, cols], ga))))
    acts = []
    for sc, cols in enumerate(sub_cols):
        ga, gx, gate = projected[sc]
        a, b = _lru_coeffs(ga, gx, ba_ref[:, cols], bx_ref[:, cols], lam_ref[:, cols],
                           ucs_all[sc])
        a_cum, b_cum = [piece(a, 0)], [piece(b, 0)]
        for k in range(1, seg):
            ak = piece(a, k)
            b_cum.append(ak * b_cum[-1] + piece(b, k))
            a_cum.append(ak * a_cum[-1])
        a_seg, b_seg = a_cum[-1], b_cum[-1]
        c = hc_in[0:1, cols]
        carries = [c]
        for s in range(SUBLANES):
            c = a_seg[s:s + 1] * c + b_seg[s:s + 1]
            carries.append(c)
        carry_in = jnp.concatenate(carries[:SUBLANES], axis=0)
        y = jnp.concatenate([a_cum[k] * carry_in + b_cum[k] for k in range(seg)], axis=0)
        acts.append((jax.nn.gelu(gate) * y).astype(BF16))
        h_lasts.append(jnp.broadcast_to(carries[SUBLANES], (SUBLANES, tcs)))
    for sc, cols in enumerate(sub_cols):
        out_ref[...] += _dot(acts[sc], wo_ref[cols, :])
    h_last = jnp.concatenate(h_lasts, axis=1)
    u_tail = jnp.concatenate(u_tails, axis=1)
    hc_s[j] = h_last
    hlast_ref[0] = h_last
    utail_s[j] = u_tail
    utail_ref[0] = u_tail

    @pl.when(j == n_steps - 1)
    def _():
        acc = jnp.swapaxes(out_ref[...].reshape(seg, SUBLANES, D_MODEL), 0, 1)
        out_ref[...] = acc.reshape(tm, D_MODEL)


def _rec_prompt_call(x, layer, prm, ffn_weights, *, tm, tiles_per_seq):
    n_rows = x.shape[0]
    tc, tcs = TC_RNN, TCS_RNN
    n_steps = D_RNN // tc
    n_tiles = n_rows // tm
    tail_spec = pl.BlockSpec((1, SUBLANES, tc), lambda i, j: (i, 0, j))
    tail_shape = jax.ShapeDtypeStruct((n_tiles, SUBLANES, D_RNN), F32)
    cast_in, cast_out, cast_shape = _cast_jobs(ffn_weights, layer, n_tiles * n_steps,
                                               lambda i, j: i * n_steps + j)
    return pl.pallas_call(
        functools.partial(_rec_prompt_kernel, tm=tm, tc=tc, tcs=tcs, n_steps=n_steps,
                          tiles_per_seq=tiles_per_seq),
        grid=(n_tiles, n_steps),
        in_specs=_rec_in_specs(layer, tm, tc, n_steps, stacked=False) + cast_in,
        out_specs=[pl.BlockSpec((tm, D_MODEL), lambda i, j: (i, 0)), tail_spec, tail_spec]
        + cast_out,
        out_shape=[jax.ShapeDtypeStruct((n_rows, D_MODEL), F32), tail_shape, tail_shape]
        + cast_shape,
        scratch_shapes=[pltpu.VMEM((tm, D_MODEL), BF16),
                        pltpu.VMEM((n_steps, SUBLANES, tc), F32),
                        pltpu.VMEM((n_steps, SUBLANES, tc), F32)],
        compiler_params=pltpu.CompilerParams(
            dimension_semantics=("arbitrary", "arbitrary"), vmem_limit_bytes=VMEM_LIMIT),
        name="recurrent_prompt",
    )(x, *prm, *ffn_weights)


def _rec_sample_kernel(x_ref, ng_ref, wgt_ref, wu_ref, cw_ref, cb_ref, wa_ref, ba_ref, wx_ref,
                       bx_ref, lam_ref, wo_ref, prev_ref, h0_ref, out_ref, utail_ref, ytail_ref,
                       wgt_o, wu_o, wa_o, wx_o, wo_o, h_s, *, tm, tc):
    j = pl.program_id(1)
    wgt, wu, wa, wx, wo = (w[...].astype(BF16) for w in (wgt_ref, wu_ref, wa_ref, wx_ref, wo_ref))
    for o_ref, w in zip((wgt_o, wu_o, wa_o, wx_o, wo_o), (wgt, wu, wa, wx, wo)):
        o_ref[...] = w

    @pl.when(j == 0)
    def _():
        x = x_ref[...]
        h_s[...] = _rms(x, ng_ref[...]).astype(BF16)
        out_ref[...] = x

    h = h_s[...]
    gate = _dot(h, wgt)
    u = _dot(h, wu)
    row8 = lax.broadcasted_iota(jnp.int32, (tm, tc), 0) & (GROUP_ROWS - 1)
    is_token = row8 >= STATE_ROWS
    u = jnp.where(is_token, u, _state_rows(prev_ref[...]))
    cw = cw_ref[...]
    uc = cb_ref[...] + cw[0:1] * _shifted(u, 3, 0)
    uc = uc + cw[1:2] * _shifted(u, 2, 0)
    uc = uc + cw[2:3] * _shifted(u, 1, 0)
    uc = uc + cw[3:4] * u
    ga, gx = _gate_dots(uc, wa, wx, list(range(tc // RNN_BLOCK)))
    a, b = _lru_coeffs(ga, gx, ba_ref[...], bx_ref[...], lam_ref[...], uc)
    a = jnp.where(is_token, a, 0.0)
    b = jnp.where(is_token, b, _state_rows(h0_ref[...]))
    for d in (1, 2, 4):
        m = row8 >= d
        a_sh = pltpu.roll(a, d, 0)
        b_sh = pltpu.roll(b, d, 0)
        b = jnp.where(m, a * b_sh + b, b)
        a = jnp.where(m, a * a_sh, a)
    act = (jax.nn.gelu(gate) * b).astype(BF16)
    out_ref[...] += _dot(act, wo)
    utail_ref[...] = u
    ytail_ref[...] = b


def _rec_sample_call(x, layer, prm, prev, h0, *, tm):
    n_rows = x.shape[0]
    tc = TC_RNN_SAMPLE
    n_steps = D_RNN // tc
    state_spec = pl.BlockSpec((tm, tc), lambda i, j: (i, j))
    state_shape = jax.ShapeDtypeStruct((n_rows, D_RNN), F32)
    return pl.pallas_call(
        functools.partial(_rec_sample_kernel, tm=tm, tc=tc),
        grid=(n_rows // tm, n_steps),
        in_specs=_rec_in_specs(layer, tm, tc, n_steps, stacked=True) + [
            pl.BlockSpec((None,) + prev.shape[1:3] + (tc,), lambda i, j: (layer, 0, 0, j)),
            pl.BlockSpec((None,) + h0.shape[1:3] + (tc,), lambda i, j: (layer, 0, 0, j))],
        out_specs=[pl.BlockSpec((tm, D_MODEL), lambda i, j: (i, 0)), state_spec, state_spec]
        + _rec_weight_specs(tc),
        out_shape=[jax.ShapeDtypeStruct((n_rows, D_MODEL), F32), state_shape, state_shape,
                   jax.ShapeDtypeStruct((D_MODEL, D_RNN), BF16),
                   jax.ShapeDtypeStruct((D_MODEL, D_RNN), BF16),
                   jax.ShapeDtypeStruct((D_RNN // RNN_BLOCK, RNN_BLOCK, RNN_BLOCK), BF16),
                   jax.ShapeDtypeStruct((D_RNN // RNN_BLOCK, RNN_BLOCK, RNN_BLOCK), BF16),
                   jax.ShapeDtypeStruct((D_RNN, D_MODEL), BF16)],
        scratch_shapes=[pltpu.VMEM((tm, D_MODEL), BF16)],
        compiler_params=pltpu.CompilerParams(
            dimension_semantics=("arbitrary", "arbitrary"), vmem_limit_bytes=VMEM_LIMIT),
        name="recurrent_sample",
    )(x, *prm, prev, h0)


def _kv_kernel(x_ref, ng_ref, wk_ref, wv_ref, cos_ref, slo_ref, shi_ref, k_ref, v_ref):
    h = _rms(x_ref[...], ng_ref[...]).astype(BF16)
    k = _dot(h, wk_ref[...].astype(BF16))
    k_ref[...] = _rope(k, cos_ref[...], slo_ref[...], shi_ref[...])
    v_ref[...] = _dot(h, wv_ref[...].astype(BF16))


def _kv_call(x, ng, w_k, w_v, tables, *, tm):
    n_rows = x.shape[0]
    table_tiles = tables[0].shape[0] // tm
    row = lambda i: (i, 0)
    pos = lambda i: (i % table_tiles, 0)
    fixed = lambda i: (0, 0)
    return pl.pallas_call(
        _kv_kernel,
        grid=(n_rows // tm,),
        in_specs=[
            pl.BlockSpec((tm, D_MODEL), row),
            pl.BlockSpec((1, D_MODEL), fixed),
            pl.BlockSpec((D_MODEL, KV_LANES), fixed),
            pl.BlockSpec((D_MODEL, KV_LANES), fixed),
            pl.BlockSpec((tm, LANES), pos),
            pl.BlockSpec((tm, LANES), pos),
            pl.BlockSpec((tm, LANES), pos),
        ],
        out_specs=[pl.BlockSpec((tm, KV_LANES), row), pl.BlockSpec((tm, KV_LANES), row)],
        out_shape=[jax.ShapeDtypeStruct((n_rows, KV_LANES), F32)] * 2,
        compiler_params=pltpu.CompilerParams(
            dimension_semantics=("arbitrary",), vmem_limit_bytes=VMEM_LIMIT),
        name="shared_kv",
    )(x, ng, w_k, w_v, *tables)


def _scores(q, kp, pair):
    rows = q.shape[0]
    hi_half = lax.broadcasted_iota(jnp.int32, (rows, LANES), 1) >= HEAD_DIM
    blocks, heads, keeps = [], [], []
    for gl in range(4):
        p = gl // 2
        keep = hi_half if p else jnp.logical_not(hi_half)
        qg = q[:, gl * LANES:(gl + 1) * LANES]
        blocks.append(jnp.where(keep, qg, 0.0))
        blocks.append(jnp.where(keep, pltpu.roll(qg, HEAD_DIM, 1), 0.0))
        g = 4 * pair + gl
        heads += [2 * g + p, 2 * g + 1 - p]
        keeps.append(keep)
    qs = jnp.concatenate(blocks, axis=0).astype(BF16)
    s = lax.dot_general(qs, kp, (((1,), (1,)), ((), ())), preferred_element_type=F32)
    return s * (HEAD_DIM ** -0.5), heads, keeps


def _attend(scored, vp, allowed, sink_ref, layer):
    s, heads, keeps = scored
    rows = s.shape[0] // 8
    probs = []
    for bi in range(8):
        sb = jnp.where(allowed, s[bi * rows:(bi + 1) * rows], -jnp.inf)
        sink = sink_ref[layer, heads[bi]]
        m = jnp.maximum(jnp.max(sb, axis=-1, keepdims=True), sink)
        pe = jnp.exp(sb - m)
        den = jnp.sum(pe, axis=-1, keepdims=True) + jnp.exp(sink - m)
        probs.append(pe / den)
    ps = jnp.concatenate(probs, axis=0).astype(BF16)
    o = _dot(ps, vp)
    outs = []
    for gl in range(4):
        o0 = o[(2 * gl) * rows:(2 * gl + 1) * rows]
        o1 = o[(2 * gl + 1) * rows:(2 * gl + 2) * rows]
        outs.append(jnp.where(keeps[gl], o0, pltpu.roll(o1, HEAD_DIM, 1)))
    return jnp.concatenate(outs, axis=1)


def _attn_kernel(*refs, layer, tm, tiles_per_seq, prompt, pps):
    it = iter(refs)
    sink_ref, x_ref, ng_ref, wq_ref, cos_ref, slo_ref, shi_ref, wo_ref = (next(it) for _ in range(8))
    ka_ref, va_ref, kb_ref, vb_ref = (next(it) for _ in range(4))
    out_ref = next(it)
    wq_o, wo_o = (None, None) if prompt else (next(it), next(it))
    h_s = next(it)
    i = pl.program_id(0)
    j = pl.program_id(1)
    if prompt:
        wq, wo = wq_ref, wo_ref[...]
    else:
        wq, wo = wq_ref[...].astype(BF16), wo_ref[...].astype(BF16)
        wq_o[...] = wq
        wo_o[...] = wo

    @pl.when(j == 0)
    def _():
        x = x_ref[...]
        h_s[...] = _rms(x, ng_ref[...]).astype(BF16)
        out_ref[...] = x

    h = h_s[...]
    rope = lambda v: _rope(v, cos_ref[...], slo_ref[...], shi_ref[...])
    blk = ATTN_BLOCK
    if prompt:
        first = (i % tiles_per_seq) == 0
        r = lax.broadcasted_iota(jnp.int32, (blk, 2 * blk), 0)
        kk = lax.broadcasted_iota(jnp.int32, (blk, 2 * blk), 1)
        band = (kk > r) & (kk <= r + blk)
        band0 = band & (kk >= jnp.where(first, blk, 0))
        qs = [_dot(h, wq[:, p * PAIR_W:(p + 1) * PAIR_W]) for p in range(pps)]
        scored, values, masks = [], [], []
        for p in range(pps):
            q = rope(qs[p])
            lanes = slice(p * LANES, (p + 1) * LANES)
            for qb in range(tm // blk):
                cur = slice(qb * blk, (qb + 1) * blk)
                if qb == 0:
                    k_prev, v_prev = kb_ref[:, lanes], vb_ref[:, lanes]
                else:
                    prv = slice((qb - 1) * blk, qb * blk)
                    k_prev, v_prev = ka_ref[prv, lanes], va_ref[prv, lanes]
                masks.append(band0 if qb == 0 else band)
                kp = jnp.concatenate([k_prev, ka_ref[cur, lanes]], axis=0).astype(BF16)
                values.append(
                    jnp.concatenate([v_prev, va_ref[cur, lanes]], axis=0).astype(BF16))
                scored.append(_scores(q[cur], kp, j * pps + p))
        outs = [_attend(sc, vp, allowed, sink_ref, layer)
                for sc, vp, allowed in zip(scored, values, masks)]
        per = tm // blk
        o = jnp.concatenate(
            [jnp.concatenate(outs[p * per:(p + 1) * per], axis=0) for p in range(pps)], axis=1)
    else:
        q = rope(_dot(h, wq))
        n_groups = tm // GROUP_ROWS
        t = lax.broadcasted_iota(jnp.int32, (GROUP_ROWS, 2 * blk), 0) - STATE_ROWS
        kk = lax.broadcasted_iota(jnp.int32, (GROUP_ROWS, 2 * blk), 1)
        new_t = kk - blk - STATE_ROWS
        allowed = ((kk < blk) & (kk > t)) | ((kk >= blk) & (new_t >= 0) & (new_t <= t) &
                                               (kk < blk + GROUP_ROWS))
        allowed = allowed & (t >= 0)
        pad = jnp.zeros((blk - GROUP_ROWS, LANES), F32)
        scored, values = [], []
        for b in range(n_groups):
            rows = slice(b * GROUP_ROWS, (b + 1) * GROUP_ROWS)
            kp = jnp.concatenate([ka_ref[b], kb_ref[rows, :], pad], axis=0).astype(BF16)
            values.append(jnp.concatenate([va_ref[b], vb_ref[rows, :], pad], axis=0).astype(BF16))
            scored.append(_scores(q[rows], kp, j))
        o = jnp.concatenate(
            [_attend(sc, vp, allowed, sink_ref, layer) for sc, vp in zip(scored, values)], axis=0)
    out_ref[...] += _dot(o.astype(BF16), wo)


def _attn_call(x, layer, ng, w_q, w_o, sinks, tables, ka, va, kb, vb, *, tm, tiles_per_seq, prompt):
    n_rows = x.shape[0]
    n_tiles = n_rows // tm
    pps = PAIRS_PER_STEP if prompt else 1
    row = lambda i, j: (i, 0)
    pos = lambda i, j: (i % tiles_per_seq, 0)
    if prompt:
        per = tm // ATTN_BLOCK
        cur_spec = pl.BlockSpec((tm, pps * LANES), lambda i, j: (i, j))
        prev_spec = pl.BlockSpec((ATTN_BLOCK, pps * LANES),
                                 lambda i, j: (jnp.maximum(i * per - 1, 0), j))
        kv_specs = [cur_spec, cur_spec, prev_spec, prev_spec]
    else:
        n_groups = tm // GROUP_ROWS
        cache_spec = pl.BlockSpec((n_groups, ATTN_BLOCK, LANES), lambda i, j: (0, 0, j))
        new_spec = pl.BlockSpec((tm, LANES), lambda i, j: (i, j))
        kv_specs = [cache_spec, cache_spec, new_spec, new_spec]
    wq_copy = pl.BlockSpec((D_MODEL, pps * PAIR_W), lambda i, j: (0, j))
    wo_copy = pl.BlockSpec((pps * PAIR_W, D_MODEL), lambda i, j: (j, 0))
    x_shape = jax.ShapeDtypeStruct((n_rows, D_MODEL), F32)
    if prompt:
        wq_spec, wo_spec = wq_copy, wo_copy
        out_specs, out_shape = pl.BlockSpec((tm, D_MODEL), row), x_shape
    else:
        wq_spec = pl.BlockSpec((None, D_MODEL, PAIR_W), lambda i, j: (layer, 0, j))
        wo_spec = pl.BlockSpec((None, PAIR_W, D_MODEL), lambda i, j: (layer, j, 0))
        out_specs = [pl.BlockSpec((tm, D_MODEL), row), wq_copy, wo_copy]
        out_shape = [x_shape, jax.ShapeDtypeStruct((D_MODEL, N_HEADS * HEAD_DIM), BF16),
                     jax.ShapeDtypeStruct((N_HEADS * HEAD_DIM, D_MODEL), BF16)]
    return pl.pallas_call(
        functools.partial(_attn_kernel, layer=layer, tm=tm, tiles_per_seq=tiles_per_seq,
                          prompt=prompt, pps=pps),
        grid=(n_tiles, N_PAIRS // pps),
        in_specs=[
            pl.BlockSpec(memory_space=pltpu.SMEM),
            pl.BlockSpec((tm, D_MODEL), row),
            pl.BlockSpec((None, 1, D_MODEL), lambda i, j: (layer, 0, 0)),
            wq_spec,
            pl.BlockSpec((tm, LANES), pos),
            pl.BlockSpec((tm, LANES), pos),
            pl.BlockSpec((tm, LANES), pos),
            wo_spec,
        ] + kv_specs,
        out_specs=out_specs,
        out_shape=out_shape,
        scratch_shapes=[pltpu.VMEM((tm, D_MODEL), BF16)],
        compiler_params=pltpu.CompilerParams(
            dimension_semantics=("arbitrary", "arbitrary"), vmem_limit_bytes=VMEM_LIMIT),
        name="attention_prompt" if prompt else "attention_sample",
    )(sinks, x, ng, w_q, *tables, w_o, ka, va, kb, vb)


def _rope_tables(pos):
    half = HEAD_DIM // 2
    freqs = ROPE_THETA ** (-jnp.arange(half, dtype=F32) / half)
    ang = pos.astype(F32)[:, None] * freqs[None, :]
    cos, sin = jnp.cos(ang), jnp.sin(ang)
    zero = jnp.zeros_like(sin)
    cos_t = jnp.concatenate([cos, cos] * 2, axis=1)
    sin_lo = jnp.concatenate([-sin, zero] * 2, axis=1)
    sin_hi = jnp.concatenate([zero, sin] * 2, axis=1)
    return cos_t, sin_lo, sin_hi


def kernel(x_prompt, x_sample, state_rglru_h, state_rglru_conv, state_ffn_conv, cache_window_k,
           cache_window_v, a_norm, a_w_in, a_conv_w, a_conv_b, a_gate_a_w, a_gate_a_b, a_gate_x_w,
           a_gate_x_b, a_lambda, a_w_out, kv_norm, w_k, w_v, b_norm, w_q, sinks, w_o, f_norm,
           f_w_up, f_conv_w, f_conv_b, f_w_down, final_norm):
    bp, s_len, _ = x_prompt.shape
    bs, t_len, _ = x_sample.shape
    depth = f_norm.shape[0]
    n_a = a_norm.shape[0]
    assert t_len == GROUP_ROWS - STATE_ROWS and cache_window_k.shape[1] == ATTN_BLOCK
    tiles_per_seq = s_len // TM_PROMPT
    ts = bs * GROUP_ROWS

    xp = x_prompt.reshape(bp * s_len, D_MODEL)
    xs = jnp.pad(x_sample, ((0, 0), (STATE_ROWS, 0), (0, 0))).reshape(ts, D_MODEL)
    pos_p = jnp.arange(s_len, dtype=jnp.int32)
    pos_s = jnp.tile(PAST_LEN - STATE_ROWS + jnp.arange(GROUP_ROWS, dtype=jnp.int32), bs)
    tab_p = _rope_tables(pos_p)
    tab_s = _rope_tables(pos_s)

    vec = lambda v: v.reshape(v.shape[:-1] + (1, v.shape[-1]))
    a_norm3, a_conv_b3, a_lambda3 = vec(a_norm), vec(a_conv_b), vec(a_lambda)
    a_gate_a_b3, a_gate_x_b3 = vec(a_gate_a_b), vec(a_gate_x_b)
    f_norm3, f_conv_b3, b_norm3 = vec(f_norm), vec(f_conv_b), vec(b_norm)

    def rec_params(w_gate_in, w_u_in, w_a, w_x, w_out):
        return (a_norm3, w_gate_in, w_u_in, a_conv_w, a_conv_b3, w_a, a_gate_a_b3, w_x,
                a_gate_x_b3, a_lambda3, w_out)
    cache_k = cache_window_k.reshape(bs, ATTN_BLOCK, KV_LANES)
    cache_v = cache_window_v.reshape(bs, ATTN_BLOCK, KV_LANES)
    rec_h0 = state_rglru_h[:, :, None, :]
    last = slice(tiles_per_seq - 1, None, tiles_per_seq)

    h_p, h_s, c_p, c_s, f_p, f_s = [], [], [], [], [], []
    for layer in range(depth):
        if layer < n_a:
            prm = rec_params(a_w_in, a_w_in, a_gate_a_w, a_gate_x_w, a_w_out)
            xs, ut, yt, *w_b = _rec_sample_call(xs, layer, prm, state_rglru_conv, rec_h0,
                                                tm=ts)
            c_s.append(ut.reshape(bs, GROUP_ROWS, D_RNN)[:, GROUP_ROWS - 3:])
            h_s.append(yt.reshape(bs, GROUP_ROWS, D_RNN)[:, GROUP_ROWS - 1])
            xp, ut, ht, w_up_b, w_down_b = _rec_prompt_call(
                xp, layer, rec_params(*w_b), (f_w_up, f_w_down), tm=TM_PROMPT,
                tiles_per_seq=tiles_per_seq)
            c_p.append(ut[last, SUBLANES - 3:])
            h_p.append(ht[last, 0])
        else:
            j = layer - n_a
            if j == 0:
                k_s, v_s = _kv_call(xs, vec(kv_norm), w_k, w_v, tab_s, tm=ts)
                k_p, v_p = _kv_call(xp, vec(kv_norm), w_k, w_v, tab_p, tm=TM_PROMPT)
            xs, w_q_b, w_o_b = _attn_call(xs, j, b_norm3, w_q, w_o, sinks, tab_s, cache_k, cache_v,
                                          k_s, v_s, tm=ts, tiles_per_seq=1, prompt=False)
            xp = _attn_call(xp, j, b_norm3, w_q_b, w_o_b, sinks, tab_p, k_p, v_p, k_p, v_p,
                            tm=TM_PROMPT, tiles_per_seq=tiles_per_seq, prompt=True)
        fn = vec(final_norm) if layer == depth - 1 else None
        xs, gt = _ffn_call(xs, layer, f_norm3, w_up_b, f_conv_w, f_conv_b3, w_down_b,
                           state_ffn_conv, fn, tm=ts, carry=False, tiles_per_seq=1)
        f_s.append(gt.reshape(bs, GROUP_ROWS, D_FF)[:, GROUP_ROWS - 2:])
        cast_next = (f_w_up, f_w_down) if n_a <= layer + 1 < depth else ()
        xp, gt, *w_next = _ffn_call(xp, layer, f_norm3, w_up_b, f_conv_w, f_conv_b3, w_down_b, None,
                                    fn, cast_next, tm=TM_PROMPT, carry=True,
                                    tiles_per_seq=tiles_per_seq)
        f_p.append(gt[last, SUBLANES - 2:])
        if w_next:
            w_up_b, w_down_b = w_next

    y_prompt = xp.reshape(bp, s_len, D_MODEL)
    y_sample = xs.reshape(bs, GROUP_ROWS, D_MODEL)[:, STATE_ROWS:]
    heads = lambda a: a.reshape(a.shape[0], a.shape[1], N_KV_HEADS, HEAD_DIM)
    win_p = min(ATTN_BLOCK, s_len)
    k_p3 = k_p.reshape(bp, s_len, KV_LANES)
    v_p3 = v_p.reshape(bp, s_len, KV_LANES)
    k_new = k_s.reshape(bs, GROUP_ROWS, KV_LANES)[:, STATE_ROWS:]
    v_new = v_s.reshape(bs, GROUP_ROWS, KV_LANES)[:, STATE_ROWS:]
    win_k = jnp.concatenate([cache_k[:, t_len:], k_new], axis=1)
    win_v = jnp.concatenate([cache_v[:, t_len:], v_new], axis=1)
    return (y_prompt, y_sample, jnp.stack(h_p), jnp.stack(h_s), jnp.stack(c_p), jnp.stack(c_s),
            jnp.stack(f_p), jnp.stack(f_s), heads(k_p3[:, s_len - win_p:]),
            heads(v_p3[:, s_len - win_p:]), heads(win_k), heads(win_v))
```
